```python
import jax, jax.numpy as jnp
from jax import lax
import numpy as np

D_MODEL = 1024
BATCH = 32
SEQ = 2048
DEPTH = 4

N_MIXERS = 3
N_A = (DEPTH + 2) // 3
N_B = (DEPTH + 1) // 3
N_C = DEPTH // 3

ATTN_GROUPS = ((128, 1), (512, 4), (2048, 16))
N_GROUPS = len(ATTN_GROUPS)
HEADS_PER_GROUP = 16
HEAD_DIM = 64
ROT_DIM = HEAD_DIM // 4
ROPE_THETA = 500000.0
QKV_WIDTH = 3 * N_GROUPS * HEADS_PER_GROUP * HEAD_DIM
ATTN_OUT_WIDTH = HEADS_PER_GROUP * HEAD_DIM

SC_WIDTH = 3
CONF_CONV_WIDTH = 31
FFN_DIM = 2816
FFN_CONV_WIDTH = 3
EPS_RMS = 1e-6
EPS_LN = 1e-5
NEG_INF = -1e30

kernel_name = "hybrid_dilated_attn_shortconv_conformer_convffn"


def rms_norm(x, g):
    xf = x.astype(jnp.float32)
    y = xf * lax.rsqrt(jnp.mean(xf * xf, axis=-1, keepdims=True) + EPS_RMS)
    return (y * g.astype(jnp.float32)).astype(x.dtype)


def layer_norm(x, g, b):
    xf = x.astype(jnp.float32)
    mu = jnp.mean(xf, axis=-1, keepdims=True)
    var = jnp.mean(jnp.square(xf - mu), axis=-1, keepdims=True)
    y = (xf - mu) * lax.rsqrt(var + EPS_LN)
    return (y * g.astype(jnp.float32) + b.astype(jnp.float32)).astype(x.dtype)


def causal_dwconv(x, w):
    k_width, ch = w.shape
    return lax.conv_general_dilated(
        x, w[:, None, :].astype(x.dtype), window_strides=(1,), padding=[(k_width - 1, 0)],
        dimension_numbers=("NWC", "WIO", "NWC"), feature_group_count=ch)


def rope_partial(t, cos, sin):
    half = ROT_DIM // 2
    tr = t[..., :ROT_DIM].astype(jnp.float32)
    t1, t2 = tr[..., :half], tr[..., half:]
    rot = jnp.concatenate([t1 * cos - t2 * sin, t2 * cos + t1 * sin], axis=-1)
    return jnp.concatenate([rot.astype(t.dtype), t[..., ROT_DIM:]], axis=-1)


def dilated_group_attention(q, k, v, window, dilation):
    B, S, H, Dh = q.shape
    L = S // dilation
    W = window // dilation
    nb = -(-L // W)
    Lp = nb * W

    def to_sub(t):
        t = t.reshape(B, L, dilation, H, Dh).transpose(0, 2, 1, 3, 4).reshape(B * dilation, L, H, Dh)
        t = jnp.pad(t, ((0, 0), (0, Lp - L), (0, 0), (0, 0)))
        return t.reshape(B * dilation, nb, W, H, Dh)

    def with_prev(t):
        prev = jnp.pad(t[:, :-1], ((0, 0), (1, 0), (0, 0), (0, 0), (0, 0)))
        return jnp.concatenate([prev, t], axis=2)

    qb, kb, vb = to_sub(q), to_sub(k), to_sub(v)
    kk, vv = with_prev(kb), with_prev(vb)
    scale = 1.0 / np.sqrt(Dh).astype(np.float32)
    s = jnp.einsum("bnqhd,bnkhd->bnhqk", qb, kk, preferred_element_type=jnp.float32) * scale
    qi = jnp.arange(W)[:, None]
    kj = jnp.arange(2 * W)[None, :]
    dist = qi - kj + W
    blk = jnp.arange(nb)[:, None, None]
    valid = (dist >= 0) & (dist <= W) & ((blk > 0) | (kj >= W))
    s = jnp.where(valid[None, :, None], s, NEG_INF)
    lse = jax.nn.logsumexp(s, axis=-1)
    p = jnp.exp(s - lse[..., None])
    o = jnp.einsum("bnhqk,bnkhd->bnqhd", p.astype(v.dtype), vv)
    o = o.reshape(B * dilation, Lp, H, Dh)[:, :L]
    o = o.reshape(B, dilation, L, H, Dh).transpose(0, 2, 1, 3, 4).reshape(B, S, H, Dh)
    lse = lse.transpose(0, 1, 3, 2).reshape(B * dilation, Lp, H)[:, :L]
    lse = lse.reshape(B, dilation, L, H).transpose(0, 2, 1, 3).reshape(B, S, H)
    return o, lse


def dilated_attention_mixer(x, positions, w_qkv, w_o):
    B, S, _ = x.shape
    GH = N_GROUPS * HEADS_PER_GROUP
    qkv = (x @ w_qkv).reshape(B, S, 3, GH, HEAD_DIM)
    inv_freq = ROPE_THETA ** (-jnp.arange(0, ROT_DIM, 2, dtype=jnp.float32) / ROT_DIM)
    ang = positions.astype(jnp.float32)[..., None] * inv_freq
    cos, sin = jnp.cos(ang)[:, :, None, :], jnp.sin(ang)[:, :, None, :]
    q = rope_partial(qkv[:, :, 0], cos, sin)
    k = rope_partial(qkv[:, :, 1], cos, sin)
    v = qkv[:, :, 2]
    outs, lses = [], []
    for g, (window, dilation) in enumerate(ATTN_GROUPS):
        sl = slice(g * HEADS_PER_GROUP, (g + 1) * HEADS_PER_GROUP)
        o, l = dilated_group_attention(q[:, :, sl], k[:, :, sl], v[:, :, sl], window, dilation)
        outs.append(o)
        lses.append(l)
    alpha = jax.nn.softmax(jnp.stack(lses, axis=0), axis=0)
    o = jnp.sum(alpha[..., None] * jnp.stack(outs, axis=0).astype(jnp.float32), axis=0)
    return o.astype(x.dtype).reshape(B, S, ATTN_OUT_WIDTH) @ w_o


def short_conv_mixer(x, w_in, w_conv, w_out):
    b_gate, c_gate, h = jnp.split(x @ w_in, 3, axis=-1)
    return (b_gate * causal_dwconv(c_gate * h, w_conv)) @ w_out


def conformer_conv_mixer(x, w_pw1, b_pw1, w_dw, b_dw, ln_g, ln_b, w_pw2, b_pw2):
    a, g = jnp.split(x @ w_pw1 + b_pw1, 2, axis=-1)
    h = a * jax.nn.sigmoid(g)
    h = causal_dwconv(h, w_dw) + b_dw
    h = jax.nn.silu(layer_norm(h, ln_g, ln_b))
    return h @ w_pw2 + b_pw2


def conv_ffn(x, w_in, w_conv, w_out):
    u = causal_dwconv(x @ w_in, w_conv)
    g, v = jnp.split(u, 2, axis=-1)
    return (jax.nn.silu(g) * v) @ w_out


def setup_inputs(seed: int = 0) -> dict:
    key = jax.random.key(seed)
    ks = jax.random.split(key, 24)
    f32 = jnp.float32
    D, F = D_MODEL, FFN_DIM

    def nrm(k, shape, scale):
        return jax.random.normal(k, shape, f32) * scale

    def gain(k, shape):
        return 1.0 + 0.1 * jax.random.normal(k, shape, f32)

    return {
        "x": jax.random.normal(ks[0], (BATCH, SEQ, D), f32),
        "positions": jnp.broadcast_to(jnp.arange(SEQ, dtype=jnp.int32)[None, :], (BATCH, SEQ)),
        "mix_norm_pre": gain(ks[1], (DEPTH, D)),
        "mix_norm_post": gain(ks[2], (DEPTH, D)),
        "ffn_norm_pre": gain(ks[3], (DEPTH, D)),
        "ffn_norm_post": gain(ks[4], (DEPTH, D)),
        "attn_w_qkv": nrm(ks[5], (N_A, D, QKV_WIDTH), D ** -0.5),
        "attn_w_o": nrm(ks[6], (N_A, ATTN_OUT_WIDTH, D), ATTN_OUT_WIDTH ** -0.5),
        "sc_w_in": nrm(ks[7], (N_B, D, 3 * D), D ** -0.5),
        "sc_w_conv": nrm(ks[8], (N_B, SC_WIDTH, D), SC_WIDTH ** -0.5),
        "sc_w_out": nrm(ks[9], (N_B, D, D), D ** -0.5),
        "cc_w_pw1": nrm(ks[10], (N_C, D, 2 * D), D ** -0.5),
        "cc_b_pw1": nrm(ks[11], (N_C, 2 * D), 0.01),
        "cc_w_dw": nrm(ks[12], (N_C, CONF_CONV_WIDTH, D), CONF_CONV_WIDTH ** -0.5),
        "cc_b_dw": nrm(ks[13], (N_C, D), 0.01),
        "cc_ln_g": gain(ks[14], (N_C, D)),
        "cc_ln_b": nrm(ks[15], (N_C, D), 0.01),
        "cc_w_pw2": nrm(ks[16], (N_C, D, D), D ** -0.5),
        "cc_b_pw2": nrm(ks[17], (N_C, D), 0.01),
        "ffn_w_in": nrm(ks[18], (DEPTH, D, 2 * F), D ** -0.5),
        "ffn_w_conv": nrm(ks[19], (DEPTH, FFN_CONV_WIDTH, 2 * F), FFN_CONV_WIDTH ** -0.5),
        "ffn_w_out": nrm(ks[20], (DEPTH, F, D), F ** -0.5),
    }


def reference(x, positions, mix_norm_pre, mix_norm_post, ffn_norm_pre, ffn_norm_post,
              attn_w_qkv, attn_w_o, sc_w_in, sc_w_conv, sc_w_out,
              cc_w_pw1, cc_b_pw1, cc_w_dw, cc_b_dw, cc_ln_g, cc_ln_b, cc_w_pw2, cc_b_pw2,
              ffn_w_in, ffn_w_conv, ffn_w_out):
    h = x
    for i in range(DEPTH):
        kind, j = i % N_MIXERS, i // N_MIXERS
        a = rms_norm(h, mix_norm_pre[i])
        if kind == 0:
            m = dilated_attention_mixer(a, positions, attn_w_qkv[j], attn_w_o[j])
        elif kind == 1:
            m = short_conv_mixer(a, sc_w_in[j], sc_w_conv[j], sc_w_out[j])
        else:
            m = conformer_conv_mixer(a, cc_w_pw1[j], cc_b_pw1[j], cc_w_dw[j], cc_b_dw[j],
                                     cc_ln_g[j], cc_ln_b[j], cc_w_pw2[j], cc_b_pw2[j])
        h = h + rms_norm(m, mix_norm_post[i])
        f = conv_ffn(rms_norm(h, ffn_norm_pre[i]), ffn_w_in[i], ffn_w_conv[i], ffn_w_out[i])
        h = h + rms_norm(f, ffn_norm_post[i])
    return h
```

```python
import functools

import jax
import jax.numpy as jnp
from jax import lax
from jax.experimental import pallas as pl
from jax.experimental.pallas import tpu as pltpu

F32 = jnp.float32
BF16 = jnp.bfloat16

D_MODEL = 1024
DEPTH = 4
N_MIXERS = 3
ATTN_GROUPS = ((128, 1), (512, 4), (2048, 16))
DILATIONS = tuple(d for _, d in ATTN_GROUPS)
N_GROUPS = len(ATTN_GROUPS)
HEADS_PER_GROUP = 16
HEAD_DIM = 64
ROT_DIM = HEAD_DIM // 4
ROPE_THETA = 500000.0
GROUP_WIDTH = HEADS_PER_GROUP * HEAD_DIM
WINDOW_STEPS = 128
CONF_CONV_WIDTH = 31
FFN_DIM = 2816
EPS_RMS = 1e-6
EPS_LN = 1e-5
NEG_INF = -1e30

LANES = 128
SUBLANES = 8
FFN_CHUNK = 256
N_FFN_CHUNKS = FFN_DIM // FFN_CHUNK
CONF_HALO = 32
CONF_ROW_BLOCK = 16
VMEM_LIMIT = 60 * 1024 * 1024


def _rms(x, g):
    return x * lax.rsqrt(jnp.mean(x * x, axis=-1, keepdims=True) + EPS_RMS) * g


def _sigmoid(x):
    return 1.0 / (1.0 + jnp.exp(-x))


def _const_spec(shape):
    n = len(shape)
    return pl.BlockSpec(shape, lambda b, s: (0,) * n, pipeline_mode=pl.Buffered(1))


def _tile_spec(tm, width):
    return pl.BlockSpec((1, tm, width), lambda b, s: (b, s, 0))


def _params():
    return pltpu.CompilerParams(
        dimension_semantics=("arbitrary", "arbitrary"), vmem_limit_bytes=VMEM_LIMIT)


def _ffn_kernel(h_ref, gpre_ref, gpost_ref, win_ref, wconv_ref, wout_ref, o_ref,
                xn_ref, ybuf_ref, carry_ref, act_ref, *, tm):
    x = h_ref[0]
    xn_ref[...] = _rms(x, gpre_ref[...]).astype(BF16)

    @pl.when(pl.program_id(1) == 0)
    def _():
        carry_ref[...] = jnp.zeros_like(carry_ref)

    def chunk(c, carry):
        taps = wconv_ref[c]
        u = []
        for half in range(2):
            y = jnp.dot(xn_ref[...], win_ref[c, half], preferred_element_type=F32)
            ybuf_ref[half, 0:SUBLANES, :] = carry_ref[c, half]
            ybuf_ref[half, SUBLANES:SUBLANES + tm, :] = y
            carry_ref[c, half] = ybuf_ref[half, tm:tm + SUBLANES, :]
            u.append(taps[half, 0:1] * ybuf_ref[half, SUBLANES - 2:SUBLANES - 2 + tm, :]
                     + taps[half, 1:2] * ybuf_ref[half, SUBLANES - 1:SUBLANES - 1 + tm, :]
                     + taps[half, 2:3] * y)
        act_ref[c] = (u[0] * _sigmoid(u[0]) * u[1]).astype(BF16)
        return carry

    lax.fori_loop(0, N_FFN_CHUNKS, chunk, 0)
    f = jnp.dot(act_ref[0], wout_ref[0], preferred_element_type=F32)
    for c in range(1, N_FFN_CHUNKS):
        f = f + jnp.dot(act_ref[c], wout_ref[c], preferred_element_type=F32)
    o_ref[0] = x + _rms(f, gpost_ref[...])


def _ffn_layer(h, gpre, gpost, win, wconv, wout, tm=512):
    B, S, D = h.shape
    return pl.pallas_call(
        functools.partial(_ffn_kernel, tm=tm),
        grid=(B, S // tm),
        in_specs=[_tile_spec(tm, D), _const_spec((1, D)), _const_spec((1, D)),
                  _const_spec(win.shape), _const_spec(wconv.shape), _const_spec(wout.shape)],
        out_specs=_tile_spec(tm, D),
        out_shape=jax.ShapeDtypeStruct(h.shape, F32),
        scratch_shapes=[pltpu.VMEM((tm, D), BF16),
                        pltpu.VMEM((2, tm + SUBLANES, FFN_CHUNK), F32),
                        pltpu.VMEM((N_FFN_CHUNKS, 2, SUBLANES, FFN_CHUNK), F32),
                        pltpu.VMEM((N_FFN_CHUNKS, tm, FFN_CHUNK), BF16)],
        compiler_params=_params(),
        name="conv_ffn",
    )(h, gpre, gpost, win, wconv, wout)


def _sc_kernel(h_ref, gpre_ref, gpost_ref, win_ref, wconv_ref, wout_ref, o_ref, ubuf_ref, *, tm):
    D = D_MODEL
    x = h_ref[0]
    xn = _rms(x, gpre_ref[...]).astype(BF16)

    @pl.when(pl.program_id(1) == 0)
    def _():
        ubuf_ref[0:SUBLANES, :] = jnp.zeros((SUBLANES, D), F32)

    p = jnp.dot(xn, win_ref[...], preferred_element_type=F32)
    u = p[:, D:2 * D] * p[:, 2 * D:]
    ubuf_ref[SUBLANES:SUBLANES + tm, :] = u
    taps = wconv_ref[...]
    conv = (taps[0:1] * ubuf_ref[SUBLANES - 2:SUBLANES - 2 + tm, :]
            + taps[1:2] * ubuf_ref[SUBLANES - 1:SUBLANES - 1 + tm, :]
            + taps[2:3] * u)
    ubuf_ref[0:SUBLANES, :] = ubuf_ref[tm:tm + SUBLANES, :]
    z = (p[:, :D] * conv).astype(BF16)
    m = jnp.dot(z, wout_ref[...], preferred_element_type=F32)
    o_ref[0] = x + _rms(m, gpost_ref[...])


def _sc_layer(h, gpre, gpost, win, wconv, wout, tm=512):
    B, S, D = h.shape
    return pl.pallas_call(
        functools.partial(_sc_kernel, tm=tm),
        grid=(B, S // tm),
        in_specs=[_tile_spec(tm, D), _const_spec((1, D)), _const_spec((1, D)),
                  _const_spec(win.shape), _const_spec(wconv.shape), _const_spec(wout.shape)],
        out_specs=_tile_spec(tm, D),
        out_shape=jax.ShapeDtypeStruct(h.shape, F32),
        scratch_shapes=[pltpu.VMEM((tm + SUBLANES, D), F32)],
        compiler_params=_params(),
        name="short_conv",
    )(h, gpre, gpost, win, wconv, wout)


def _cc_kernel(h_ref, gpre_ref, gpost_ref, w1_ref, b1_ref, wdw_ref, bdw_ref, lng_ref, lnb_ref,
               w2_ref, b2_ref, o_ref, sh_ref, sw_ref, *, tm):
    D = D_MODEL
    K = CONF_CONV_WIDTH
    ext = tm + CONF_HALO - SUBLANES
    x = h_ref[0]
    xn = _rms(x, gpre_ref[...]).astype(BF16)

    @pl.when(pl.program_id(1) == 0)
    def _():
        sh_ref[0, 0:CONF_HALO, :] = jnp.zeros((CONF_HALO, D), F32)

    p = jnp.dot(xn, w1_ref[...], preferred_element_type=F32) + b1_ref[...]
    sh_ref[0, CONF_HALO:CONF_HALO + tm, :] = p[:, :D] * _sigmoid(p[:, D:])
    for r in range(1, SUBLANES):
        sh_ref[r, 0:ext, :] = sh_ref[0, SUBLANES - r:SUBLANES - r + ext, :]

    def row_block(i, carry):
        base = pl.multiple_of(i * CONF_ROW_BLOCK, CONF_ROW_BLOCK)
        acc = jnp.zeros((CONF_ROW_BLOCK, D), F32) + bdw_ref[...]
        for k in range(K):
            shift = K - 1 - k
            q, r = divmod(shift, SUBLANES)
            off = CONF_HALO - SUBLANES * q if r == 0 else CONF_HALO - SUBLANES - SUBLANES * q
            acc = acc + wdw_ref[k:k + 1, :] * sh_ref[r, pl.ds(base + off, CONF_ROW_BLOCK), :]
        mu = jnp.mean(acc, axis=-1, keepdims=True)
        cen = acc - mu
        var = jnp.mean(cen * cen, axis=-1, keepdims=True)
        ln = cen * lax.rsqrt(var + EPS_LN) * lng_ref[...] + lnb_ref[...]
        sw_ref[pl.ds(base, CONF_ROW_BLOCK), :] = (ln * _sigmoid(ln)).astype(BF16)
        return carry

    lax.fori_loop(0, tm // CONF_ROW_BLOCK, row_block, 0)
    sh_ref[0, 0:CONF_HALO, :] = sh_ref[0, tm:tm + CONF_HALO, :]
    m = jnp.dot(sw_ref[...], w2_ref[...], preferred_element_type=F32) + b2_ref[...]
    o_ref[0] = x + _rms(m, gpost_ref[...])


def _cc_layer(h, gpre, gpost, w1, b1, wdw, bdw, lng, lnb, w2, b2, tm=256):
    B, S, D = h.shape
    vec = _const_spec((1, D))
    return pl.pallas_call(
        functools.partial(_cc_kernel, tm=tm),
        grid=(B, S // tm),
        in_specs=[_tile_spec(tm, D), vec, vec, _const_spec(w1.shape), _const_spec(b1.shape),
                  _const_spec(wdw.shape), vec, vec, vec, _const_spec(w2.shape), vec],
        out_specs=_tile_spec(tm, D),
        out_shape=jax.ShapeDtypeStruct(h.shape, F32),
        scratch_shapes=[pltpu.VMEM((SUBLANES, tm + CONF_HALO, D), F32),
                        pltpu.VMEM((tm, D), BF16)],
        compiler_params=_params(),
        name="conformer_conv",
    )(h, gpre, gpost, w1, b1, wdw, bdw, lng, lnb, w2, b2)


def _rope_table_kernel(pos_ref, invf_ref, sign_ref, cos_ref, sin_ref):
    ang = pos_ref[0] * invf_ref[...]
    cos_ref[0] = jnp.cos(ang)
    sin_ref[0] = jnp.sin(ang) * sign_ref[...]


def _rope_tables(pos_b, invf_lane, sign_lane, tm=512):
    B, S, _ = pos_b.shape
    out = jax.ShapeDtypeStruct(pos_b.shape, F32)
    return pl.pallas_call(
        _rope_table_kernel,
        grid=(B, S // tm),
        in_specs=[_tile_spec(tm, LANES), _const_spec((1, LANES)), _const_spec((1, LANES))],
        out_specs=[_tile_spec(tm, LANES), _tile_spec(tm, LANES)],
        out_shape=[out, out],
        compiler_params=_params(),
        name="rope_tables",
    )(pos_b, invf_lane, sign_lane)


def _qkv_kernel(h_ref, g_ref, cos_ref, sin_ref, w_ref, o0_ref, o1_ref, o2_ref,
                a_ref, ap_ref, cp_ref, sp_ref, *, tm):
    n_lane_blocks = D_MODEL // LANES
    a_nat = _rms(h_ref[0], g_ref[...])
    for j in range(n_lane_blocks):
        a_ref[j] = a_nat[:, j * LANES:(j + 1) * LANES]
    lane = lax.broadcasted_iota(jnp.int32, (1, LANES), 1)
    first_half = (lane % HEAD_DIM) < (ROT_DIM // 2)
    for g, (d, o_ref) in enumerate(zip(DILATIONS, (o0_ref, o1_ref, o2_ref))):
        rows = tm // d
        if d == 1:
            a = a_nat.astype(BF16)
            cos, sin = cos_ref[0], sin_ref[0]
        else:
            for r in range(d):
                dst = slice(r * rows, (r + 1) * rows)
                for j in range(n_lane_blocks):
                    ap_ref[dst, j * LANES:(j + 1) * LANES] = (
                        a_ref[j, pl.ds(r, rows, stride=d), :].astype(BF16))
                cp_ref[dst, :] = cos_ref[0, pl.ds(r, rows, stride=d), :]
                sp_ref[dst, :] = sin_ref[0, pl.ds(r, rows, stride=d), :]
            a = ap_ref[...]
            cos, sin = cp_ref[...], sp_ref[...]
        y = jnp.dot(a, w_ref[g], preferred_element_type=F32)
        blocks = []
        for j in range(3 * GROUP_WIDTH // LANES):
            blk = y[:, j * LANES:(j + 1) * LANES]
            if j < 2 * GROUP_WIDTH // LANES:
                partner = jnp.where(first_half,
                                    pltpu.roll(blk, LANES - ROT_DIM // 2, 1),
                                    pltpu.roll(blk, ROT_DIM // 2, 1))
                blk = blk * cos + partner * sin
            blocks.append(blk.astype(BF16))
        out = jnp.concatenate(blocks, axis=1)
        for r in range(d):
            o_ref[0, r] = out[r * rows:(r + 1) * rows, :]


def _qkv_layer(h, g, cos_t, sin_t, w, tm=256):
    B, S, D = h.shape
    W3 = 3 * GROUP_WIDTH
    out_shapes = [jax.ShapeDtypeStruct((B, d, S // d, W3), BF16) for d in DILATIONS]
    out_specs = [pl.BlockSpec((1, d, tm // d, W3), lambda b, s: (b, 0, s, 0)) for d in DILATIONS]
    outs = pl.pallas_call(
        functools.partial(_qkv_kernel, tm=tm),
        grid=(B, S // tm),
        in_specs=[_tile_spec(tm, D), _const_spec((1, D)), _tile_spec(tm, LANES),
                  _tile_spec(tm, LANES), _const_spec(w.shape)],
        out_specs=out_specs,
        out_shape=out_shapes,
        scratch_shapes=[pltpu.VMEM((D // LANES, tm, LANES), F32), pltpu.VMEM((tm, D), BF16),
                        pltpu.VMEM((tm, LANES), F32), pltpu.VMEM((tm, LANES), F32)],
        compiler_params=_params(),
        name="qkv_rope",
    )(h, g, cos_t, sin_t, w)
    return [o.reshape(B, S, W3) for o in outs]


def _attn_kernel(*refs, seq):
    qkv_refs = refs[:3 * N_GROUPS]
    o_ref = refs[3 * N_GROUPS]
    acc_ref, m_ref, l_ref = refs[3 * N_GROUPS + 1:]
    W = WINDOW_STEPS
    lane = lax.broadcasted_iota(jnp.int32, (1, LANES), 1)
    head0 = lane < HEAD_DIM
    q_iota = lax.broadcasted_iota(jnp.int32, (W, 1), 0)
    k_iota = lax.broadcasted_iota(jnp.int32, (1, 2 * W), 1)

    for g, d in enumerate(DILATIONS):
        q_ref, k_ref, v_ref = qkv_refs[3 * g:3 * g + 3]
        sub_len = seq // d
        blocks_per_sub = sub_len // W

        def q_block(i, carry, q_ref=q_ref, k_ref=k_ref, v_ref=v_ref, d=d, g=g,
                    sub_len=sub_len, blocks_per_sub=blocks_per_sub):
            base = pl.multiple_of(i * W, W)
            kstart = pl.multiple_of(jnp.maximum(base - W, 0), W)
            r = i // blocks_per_sub
            n = i % blocks_per_sub
            q = q_ref[0, pl.ds(base, W), :]
            kk = k_ref[0, pl.ds(kstart, 2 * W), :]
            vv = v_ref[0, pl.ds(kstart, 2 * W), :]
            qrow = base + q_iota
            krow = kstart + k_iota
            valid = (krow >= r * sub_len) & (krow <= qrow) & (qrow - krow <= W)
            outs, ms, ls = [], [], []
            for hd in range(2):
                qh = jnp.where(head0 if hd == 0 else jnp.logical_not(head0), q, jnp.zeros_like(q))
                s = lax.dot_general(qh, kk, (((1,), (1,)), ((), ())), preferred_element_type=F32)
                s = jnp.where(valid, s, NEG_INF)
                m = jnp.max(s, axis=1, keepdims=True)
                p = jnp.exp(s - m)
                ls.append(jnp.sum(p, axis=1, keepdims=True))
                ms.append(m)
                outs.append(jnp.dot(p.astype(BF16), vv, preferred_element_type=F32))
            t0 = n * (W * d) + r
            dst = pl.ds(t0, W, stride=d) if d > 1 else pl.ds(t0, W)
            acc_ref[g, dst, :] = jnp.where(head0, outs[0], outs[1])
            m_ref[g, dst, :] = jnp.where(head0, ms[0], ms[1])
            l_ref[g, dst, :] = jnp.where(head0, ls[0], ls[1])
            return carry

        lax.fori_loop(0, seq // W, q_block, 0)

    def merge(i, carry):
        rows = pl.ds(pl.multiple_of(i * W, W), W)
        m_all = [m_ref[g, rows, :] for g in range(N_GROUPS)]
        m_max = functools.reduce(jnp.maximum, m_all)
        num = jnp.zeros((W, LANES), F32)
        den = jnp.zeros((W, LANES), F32)
        for g in range(N_GROUPS):
            wgt = jnp.exp(m_all[g] - m_max)
            num = num + wgt * acc_ref[g, rows, :]
            den = den + wgt * l_ref[g, rows, :]
        o_ref[0, rows, :] = (num / den).astype(BF16)
        return carry

    lax.fori_loop(0, seq // W, merge, 0)


def _attn_layer(qkvs):
    B, S, _ = qkvs[0].shape
    n_pairs = GROUP_WIDTH // LANES
    in_specs, args = [], []
    for arr in qkvs:
        for which in range(3):
            in_specs.append(pl.BlockSpec(
                (1, S, LANES), lambda b, p, which=which: (b, 0, which * n_pairs + p)))
            args.append(arr)
    return pl.pallas_call(
        functools.partial(_attn_kernel, seq=S),
        grid=(B, n_pairs),
        in_specs=in_specs,
        out_specs=pl.BlockSpec((1, S, LANES), lambda b, p: (b, 0, p)),
        out_shape=jax.ShapeDtypeStruct((B, S, GROUP_WIDTH), BF16),
        scratch_shapes=[pltpu.VMEM((N_GROUPS, S, LANES), F32)] * 3,
        compiler_params=_params(),
        name="dilated_attention",
    )(*args)


def _oproj_kernel(h_ref, o_ref, w_ref, gpost_ref, out_ref):
    m = jnp.dot(o_ref[0], w_ref[...], preferred_element_type=F32)
    out_ref[0] = h_ref[0] + _rms(m, gpost_ref[...])


def _oproj_layer(h, o, w, gpost, tm=512):
    B, S, D = h.shape
    return pl.pallas_call(
        _oproj_kernel,
        grid=(B, S // tm),
        in_specs=[_tile_spec(tm, D), _tile_spec(tm, GROUP_WIDTH), _const_spec(w.shape),
                  _const_spec((1, D))],
        out_specs=_tile_spec(tm, D),
        out_shape=jax.ShapeDtypeStruct(h.shape, F32),
        compiler_params=_params(),
        name="attn_out_proj",
    )(h, o, w, gpost)


def _rope_lane_constants():
    inv_freq = ROPE_THETA ** (-jnp.arange(0, ROT_DIM, 2, dtype=F32) / ROT_DIM)
    dim = jnp.arange(LANES) % HEAD_DIM
    half = ROT_DIM // 2
    invf = jnp.where(dim < ROT_DIM, inv_freq[dim % half], 0.0).astype(F32)
    sign = jnp.where(dim < half, -1.0, 1.0).astype(F32)
    return invf[None, :], sign[None, :]


def _prep_qkv_weight(w):
    D = w.shape[0]
    w = w.reshape(D, 3, N_GROUPS, GROUP_WIDTH)
    scale = jnp.array([HEAD_DIM ** -0.5, 1.0, 1.0], F32)[None, :, None, None]
    w = (w * scale).transpose(2, 0, 1, 3).reshape(N_GROUPS, D, 3 * GROUP_WIDTH)
    return w.astype(BF16)


def kernel(x, positions, mix_norm_pre, mix_norm_post, ffn_norm_pre, ffn_norm_post,
           attn_w_qkv, attn_w_o, sc_w_in, sc_w_conv, sc_w_out,
           cc_w_pw1, cc_b_pw1, cc_w_dw, cc_b_dw, cc_ln_g, cc_ln_b, cc_w_pw2, cc_b_pw2,
           ffn_w_in, ffn_w_conv, ffn_w_out):
    B, S, D = x.shape
    row = lambda v: v.reshape(1, -1)

    ffn_win = ffn_w_in.astype(BF16).reshape(DEPTH, D, 2, N_FFN_CHUNKS, FFN_CHUNK).transpose(0, 3, 2, 1, 4)
    ffn_wconv = ffn_w_conv.reshape(DEPTH, 3, 2, N_FFN_CHUNKS, FFN_CHUNK).transpose(0, 3, 2, 1, 4)
    ffn_wconv = jnp.pad(ffn_wconv, ((0, 0), (0, 0), (0, 0), (0, SUBLANES - 3), (0, 0)))
    ffn_wout = ffn_w_out.astype(BF16).reshape(DEPTH, N_FFN_CHUNKS, FFN_CHUNK, D)

    invf_lane, sign_lane = _rope_lane_constants()
    pos_b = jnp.broadcast_to(positions.astype(F32)[:, :, None], (B, S, LANES))
    cos_t, sin_t = _rope_tables(pos_b, invf_lane, sign_lane)

    h = x
    for i in range(DEPTH):
        kind, j = i % N_MIXERS, i // N_MIXERS
        gpre, gpost = row(mix_norm_pre[i]), row(mix_norm_post[i])
        if kind == 0:
            qkvs = _qkv_layer(h, gpre, cos_t, sin_t, _prep_qkv_weight(attn_w_qkv[j]))
            o = _attn_layer(qkvs)
            h = _oproj_layer(h, o, attn_w_o[j].astype(BF16), gpost)
        elif kind == 1:
            h = _sc_layer(h, gpre, gpost, sc_w_in[j].astype(BF16),
                          jnp.pad(sc_w_conv[j], ((0, SUBLANES - 3), (0, 0))),
                          sc_w_out[j].astype(BF16))
        else:
            h = _cc_layer(h, gpre, gpost, cc_w_pw1[j].astype(BF16), row(cc_b_pw1[j]),
                          jnp.pad(cc_w_dw[j], ((0, 1), (0, 0))), row(cc_b_dw[j]),
                          row(cc_ln_g[j]), row(cc_ln_b[j]), cc_w_pw2[j].astype(BF16),
                          row(cc_b_pw2[j]))
        h = _ffn_layer(h, row(ffn_norm_pre[i]), row(ffn_norm_post[i]),
                       ffn_win[i], ffn_wconv[i], ffn_wout[i])
    return h
```

```python
import functools

import jax
import jax.numpy as jnp
from jax import lax
from jax.experimental import pallas as pl
from jax.experimental.pallas import tpu as pltpu

F32 = jnp.float32
BF16 = jnp.bfloat16

D_MODEL = 1024
DEPTH = 4
N_MIXERS = 3
ATTN_GROUPS = ((128, 1), (512, 4), (2048, 16))
DILATIONS = tuple(d for _, d in ATTN_GROUPS)
N_GROUPS = len(ATTN_GROUPS)
HEADS_PER_GROUP = 16
HEAD_DIM = 64
ROT_DIM = HEAD_DIM // 4
ROPE_THETA = 500000.0
GROUP_WIDTH = HEADS_PER_GROUP * HEAD_DIM
WINDOW_STEPS = 128
CONF_CONV_WIDTH = 31
FFN_DIM = 2816
EPS_RMS = 1e-6
EPS_LN = 1e-5
NEG_INF = -1e30

LANES = 128
SUBLANES = 8
FFN_CHUNK = 256
N_FFN_CHUNKS = FFN_DIM // FFN_CHUNK
CONF_HALO = 32
CONF_ROW_BLOCK = 16
VMEM_LIMIT = 60 * 1024 * 1024


def _rms(x, g):
    return x * lax.rsqrt(jnp.mean(x * x, axis=-1, keepdims=True) + EPS_RMS) * g


def _sigmoid(x):
    return 1.0 / (1.0 + jnp.exp(-x))


def _const_spec(shape):
    n = len(shape)
    return pl.BlockSpec(shape, lambda b, s: (0,) * n, pipeline_mode=pl.Buffered(1))


def _tile_spec(tm, width):
    return pl.BlockSpec((1, tm, width), lambda b, s: (b, s, 0))


def _params():
    return pltpu.CompilerParams(
        dimension_semantics=("arbitrary", "arbitrary"), vmem_limit_bytes=VMEM_LIMIT)


def _ffn_kernel(h_ref, gpre_ref, gpost_ref, win_ref, wconv_ref, wout_ref, o_ref,
                xn_ref, ybuf_ref, carry_ref, act_ref, *, tm):
    x = h_ref[0]
    xn_ref[...] = _rms(x, gpre_ref[...]).astype(BF16)

    @pl.when(pl.program_id(1) == 0)
    def _():
        carry_ref[...] = jnp.zeros_like(carry_ref)

    def chunk(c, carry):
        taps = wconv_ref[c]
        u = []
        for half in range(2):
            y = jnp.dot(xn_ref[...], win_ref[c, half], preferred_element_type=F32)
            ybuf_ref[half, 0:SUBLANES, :] = carry_ref[c, half]
            ybuf_ref[half, SUBLANES:SUBLANES + tm, :] = y
            carry_ref[c, half] = ybuf_ref[half, tm:tm + SUBLANES, :]
            u.append(taps[half, 0:1] * ybuf_ref[half, SUBLANES - 2:SUBLANES - 2 + tm, :]
                     + taps[half, 1:2] * ybuf_ref[half, SUBLANES - 1:SUBLANES - 1 + tm, :]
                     + taps[half, 2:3] * y)
        act_ref[c] = (u[0] * _sigmoid(u[0]) * u[1]).astype(BF16)
        return carry

    lax.fori_loop(0, N_FFN_CHUNKS, chunk, 0, unroll=True)
    f = jnp.dot(act_ref[0], wout_ref[0], preferred_element_type=F32)
    for c in range(1, N_FFN_CHUNKS):
        f = f + jnp.dot(act_ref[c], wout_ref[c], preferred_element_type=F32)
    o_ref[0] = x + _rms(f, gpost_ref[...])


def _ffn_layer(h, gpre, gpost, win, wconv, wout, tm=512):
    B, S, D = h.shape
    return pl.pallas_call(
        functools.partial(_ffn_kernel, tm=tm),
        grid=(B, S // tm),
        in_specs=[_tile_spec(tm, D), _const_spec((1, D)), _const_spec((1, D)),
                  _const_spec(win.shape), _const_spec(wconv.shape), _const_spec(wout.shape)],
        out_specs=_tile_spec(tm, D),
        out_shape=jax.ShapeDtypeStruct(h.shape, F32),
        scratch_shapes=[pltpu.VMEM((tm, D), BF16),
                        pltpu.VMEM((2, tm + SUBLANES, FFN_CHUNK), F32),
                        pltpu.VMEM((N_FFN_CHUNKS, 2, SUBLANES, FFN_CHUNK), F32),
                        pltpu.VMEM((N_FFN_CHUNKS, tm, FFN_CHUNK), BF16)],
        compiler_params=_params(),
        name="conv_ffn",
    )(h, gpre, gpost, win, wconv, wout)


def _sc_kernel(h_ref, gpre_ref, gpost_ref, win_ref, wconv_ref, wout_ref, o_ref, ubuf_ref, *, tm):
    D = D_MODEL
    x = h_ref[0]
    xn = _rms(x, gpre_ref[...]).astype(BF16)

    @pl.when(pl.program_id(1) == 0)
    def _():
        ubuf_ref[0:SUBLANES, :] = jnp.zeros((SUBLANES, D), F32)

    p = jnp.dot(xn, win_ref[...], preferred_element_type=F32)
    u = p[:, D:2 * D] * p[:, 2 * D:]
    ubuf_ref[SUBLANES:SUBLANES + tm, :] = u
    taps = wconv_ref[...]
    conv = (taps[0:1] * ubuf_ref[SUBLANES - 2:SUBLANES - 2 + tm, :]
            + taps[1:2] * ubuf_ref[SUBLANES - 1:SUBLANES - 1 + tm, :]
            + taps[2:3] * u)
    ubuf_ref[0:SUBLANES, :] = ubuf_ref[tm:tm + SUBLANES, :]
    z = (p[:, :D] * conv).astype(BF16)
    m = jnp.dot(z, wout_ref[...], preferred_element_type=F32)
    o_ref[0] = x + _rms(m, gpost_ref[...])


def _sc_layer(h, gpre, gpost, win, wconv, wout, tm=512):
    B, S, D = h.shape
    return pl.pallas_call(
        functools.partial(_sc_kernel, tm=tm),
        grid=(B, S // tm),
        in_specs=[_tile_spec(tm, D), _const_spec((1, D)), _const_spec((1, D)),
                  _const_spec(win.shape), _const_spec(wconv.shape), _const_spec(wout.shape)],
        out_specs=_tile_spec(tm, D),
        out_shape=jax.ShapeDtypeStruct(h.shape, F32),
        scratch_shapes=[pltpu.VMEM((tm + SUBLANES, D), F32)],
        compiler_params=_params(),
        name="short_conv",
    )(h, gpre, gpost, win, wconv, wout)


def _cc_kernel(h_ref, gpre_ref, gpost_ref, w1_ref, b1_ref, wdw_ref, bdw_ref, lng_ref, lnb_ref,
               w2_ref, b2_ref, o_ref, sh_ref, sw_ref, *, tm):
    D = D_MODEL
    K = CONF_CONV_WIDTH
    ext = tm + CONF_HALO - SUBLANES
    x = h_ref[0]
    xn = _rms(x, gpre_ref[...]).astype(BF16)

    @pl.when(pl.program_id(1) == 0)
    def _():
        sh_ref[0, 0:CONF_HALO, :] = jnp.zeros((CONF_HALO, D), F32)

    p = jnp.dot(xn, w1_ref[...], preferred_element_type=F32) + b1_ref[...]
    sh_ref[0, CONF_HALO:CONF_HALO + tm, :] = p[:, :D] * _sigmoid(p[:, D:])
    for r in range(1, SUBLANES):
        sh_ref[r, 0:ext, :] = sh_ref[0, SUBLANES - r:SUBLANES - r + ext, :]

    def row_block(i, carry):
        base = pl.multiple_of(i * CONF_ROW_BLOCK, CONF_ROW_BLOCK)
        acc = jnp.zeros((CONF_ROW_BLOCK, D), F32) + bdw_ref[...]
        for k in range(K):
            shift = K - 1 - k
            q, r = divmod(shift, SUBLANES)
            off = CONF_HALO - SUBLANES * q if r == 0 else CONF_HALO - SUBLANES - SUBLANES * q
            acc = acc + wdw_ref[k:k + 1, :] * sh_ref[r, pl.ds(base + off, CONF_ROW_BLOCK), :]
        mu = jnp.mean(acc, axis=-1, keepdims=True)
        cen = acc - mu
        var = jnp.mean(cen * cen, axis=-1, keepdims=True)
        ln = cen * lax.rsqrt(var + EPS_LN) * lng_ref[...] + lnb_ref[...]
        sw_ref[pl.ds(base, CONF_ROW_BLOCK), :] = (ln * _sigmoid(ln)).astype(BF16)
        return carry

    lax.fori_loop(0, tm // CONF_ROW_BLOCK, row_block, 0, unroll=4)
    sh_ref[0, 0:CONF_HALO, :] = sh_ref[0, tm:tm + CONF_HALO, :]
    m = jnp.dot(sw_ref[...], w2_ref[...], preferred_element_type=F32) + b2_ref[...]
    o_ref[0] = x + _rms(m, gpost_ref[...])


def _cc_layer(h, gpre, gpost, w1, b1, wdw, bdw, lng, lnb, w2, b2, tm=256):
    B, S, D = h.shape
    vec = _const_spec((1, D))
    return pl.pallas_call(
        functools.partial(_cc_kernel, tm=tm),
        grid=(B, S // tm),
        in_specs=[_tile_spec(tm, D), vec, vec, _const_spec(w1.shape), _const_spec(b1.shape),
                  _const_spec(wdw.shape), vec, vec, vec, _const_spec(w2.shape), vec],
        out_specs=_tile_spec(tm, D),
        out_shape=jax.ShapeDtypeStruct(h.shape, F32),
        scratch_shapes=[pltpu.VMEM((SUBLANES, tm + CONF_HALO, D), F32),
                        pltpu.VMEM((tm, D), BF16)],
        compiler_params=_params(),
        name="conformer_conv",
    )(h, gpre, gpost, w1, b1, wdw, bdw, lng, lnb, w2, b2)


def _rope_table_kernel(pos_ref, invf_ref, sign_ref, cos_ref, sin_ref):
    ang = pos_ref[0] * invf_ref[...]
    cos_ref[0] = jnp.cos(ang)
    sin_ref[0] = jnp.sin(ang) * sign_ref[...]


def _rope_tables(pos_b, invf_lane, sign_lane, tm=512):
    B, S, _ = pos_b.shape
    out = jax.ShapeDtypeStruct(pos_b.shape, F32)
    return pl.pallas_call(
        _rope_table_kernel,
        grid=(B, S // tm),
        in_specs=[_tile_spec(tm, LANES), _const_spec((1, LANES)), _const_spec((1, LANES))],
        out_specs=[_tile_spec(tm, LANES), _tile_spec(tm, LANES)],
        out_shape=[out, out],
        compiler_params=_params(),
        name="rope_tables",
    )(pos_b, invf_lane, sign_lane)


def _qkv_kernel(h_ref, g_ref, cos_ref, sin_ref, w_ref, o0_ref, o1_ref, o2_ref,
                a_ref, ap_ref, cp_ref, sp_ref, *, tm):
    n_lane_blocks = D_MODEL // LANES
    a_nat = _rms(h_ref[0], g_ref[...])
    for j in range(n_lane_blocks):
        a_ref[j] = a_nat[:, j * LANES:(j + 1) * LANES]
    lane = lax.broadcasted_iota(jnp.int32, (1, LANES), 1)
    first_half = (lane % HEAD_DIM) < (ROT_DIM // 2)
    for g, (d, o_ref) in enumerate(zip(DILATIONS, (o0_ref, o1_ref, o2_ref))):
        rows = tm // d
        if d == 1:
            a = a_nat.astype(BF16)
            cos, sin = cos_ref[0], sin_ref[0]
        else:
            for r in range(d):
                dst = slice(r * rows, (r + 1) * rows)
                for j in range(n_lane_blocks):
                    ap_ref[dst, j * LANES:(j + 1) * LANES] = (
                        a_ref[j, pl.ds(r, rows, stride=d), :].astype(BF16))
                cp_ref[dst, :] = cos_ref[0, pl.ds(r, rows, stride=d), :]
                sp_ref[dst, :] = sin_ref[0, pl.ds(r, rows, stride=d), :]
            a = ap_ref[...]
            cos, sin = cp_ref[...], sp_ref[...]
        y = jnp.dot(a, w_ref[g], preferred_element_type=F32)
        blocks = []
        for j in range(3 * GROUP_WIDTH // LANES):
            blk = y[:, j * LANES:(j + 1) * LANES]
            if j < 2 * GROUP_WIDTH // LANES:
                partner = jnp.where(first_half,
                                    pltpu.roll(blk, LANES - ROT_DIM // 2, 1),
                                    pltpu.roll(blk, ROT_DIM // 2, 1))
                blk = blk * cos + partner * sin
            blocks.append(blk.astype(BF16))
        out = jnp.concatenate(blocks, axis=1)
        for r in range(d):
            o_ref[0, r] = out[r * rows:(r + 1) * rows, :]


def _qkv_layer(h, g, cos_t, sin_t, w, tm=256):
    B, S, D = h.shape
    W3 = 3 * GROUP_WIDTH
    out_shapes = [jax.ShapeDtypeStruct((B, d, S // d, W3), BF16) for d in DILATIONS]
    out_specs = [pl.BlockSpec((1, d, tm // d, W3), lambda b, s: (b, 0, s, 0)) for d in DILATIONS]
    outs = pl.pallas_call(
        functools.partial(_qkv_kernel, tm=tm),
        grid=(B, S // tm),
        in_specs=[_tile_spec(tm, D), _const_spec((1, D)), _tile_spec(tm, LANES),
                  _tile_spec(tm, LANES), _const_spec(w.shape)],
        out_specs=out_specs,
        out_shape=out_shapes,
        scratch_shapes=[pltpu.VMEM((D // LANES, tm, LANES), F32), pltpu.VMEM((tm, D), BF16),
                        pltpu.VMEM((tm, LANES), F32), pltpu.VMEM((tm, LANES), F32)],
        compiler_params=_params(),
        name="qkv_rope",
    )(h, g, cos_t, sin_t, w)
    return [o.reshape(B, S, W3) for o in outs]


def _attn_kernel(*refs, seq):
    qkv_refs = refs[:3 * N_GROUPS]
    o_ref = refs[3 * N_GROUPS]
    acc_ref, m_ref, l_ref, bias_ref = refs[3 * N_GROUPS + 1:]
    W = WINDOW_STEPS
    lane = lax.broadcasted_iota(jnp.int32, (1, LANES), 1)
    head0 = lane < HEAD_DIM

    @pl.when((pl.program_id(0) == 0) & (pl.program_id(1) == 0))
    def _():
        qi = lax.broadcasted_iota(jnp.int32, (W, 2 * W), 0)
        kj = lax.broadcasted_iota(jnp.int32, (W, 2 * W), 1)
        dist = qi - kj + W
        in_window = (dist >= 0) & (dist <= W)
        for t, valid in enumerate((in_window, in_window & (kj >= W), kj <= qi)):
            bias_ref[t] = jnp.where(valid, 0.0, NEG_INF).astype(F32)

    for g, d in enumerate(DILATIONS):
        q_ref, k_ref, v_ref = qkv_refs[3 * g:3 * g + 3]
        blocks_per_sub = seq // d // W
        n_keys = W if blocks_per_sub == 1 else 2 * W

        def q_block(i, carry, q_ref=q_ref, k_ref=k_ref, v_ref=v_ref, d=d, g=g,
                    blocks_per_sub=blocks_per_sub, n_keys=n_keys):
            base = pl.multiple_of(i * W, W)
            r = i // blocks_per_sub
            n = i % blocks_per_sub
            if n_keys == W:
                kstart = base
                bias = bias_ref[2, :, 0:W]
            else:
                kstart = pl.multiple_of(jnp.maximum(base - W, 0), W)
                bias = bias_ref[jnp.where(i == 0, 2, jnp.where(n == 0, 1, 0))]
            q = q_ref[0, pl.ds(base, W), :]
            kk = k_ref[0, pl.ds(kstart, n_keys), :]
            vv = v_ref[0, pl.ds(kstart, n_keys), :]
            zero = jnp.zeros_like(q)
            q2 = jnp.concatenate([jnp.where(head0, q, zero), jnp.where(head0, zero, q)], axis=0)
            s2 = lax.dot_general(q2, kk, (((1,), (1,)), ((), ())), preferred_element_type=F32)
            ps, ms, ls = [], [], []
            for hd in range(2):
                s = s2[hd * W:(hd + 1) * W] + bias
                m = jnp.max(s, axis=1, keepdims=True)
                p = jnp.exp(s - m)
                ls.append(jnp.sum(p, axis=1, keepdims=True))
                ms.append(m)
                ps.append(p.astype(BF16))
            o2 = jnp.dot(jnp.concatenate(ps, axis=0), vv, preferred_element_type=F32)
            t0 = n * (W * d) + r
            dst = pl.ds(t0, W, stride=d) if d > 1 else pl.ds(t0, W)
            acc_ref[g, dst, :] = jnp.where(head0, o2[:W], o2[W:])
            m_ref[g, dst, :] = jnp.where(head0, ms[0], ms[1])
            l_ref[g, dst, :] = jnp.where(head0, ls[0], ls[1])
            return carry

        lax.fori_loop(0, seq // W, q_block, 0, unroll=16)

    def merge(i, carry):
        rows = pl.ds(pl.multiple_of(i * W, W), W)
        m_all = [m_ref[g, rows, :] for g in range(N_GROUPS)]
        m_max = functools.reduce(jnp.maximum, m_all)
        num = jnp.zeros((W, LANES), F32)
        den = jnp.zeros((W, LANES), F32)
        for g in range(N_GROUPS):
            wgt = jnp.exp(m_all[g] - m_max)
            num = num + wgt * acc_ref[g, rows, :]
            den = den + wgt * l_ref[g, rows, :]
        o_ref[0, rows, :] = (num / den).astype(BF16)
        return carry

    lax.fori_loop(0, seq // W, merge, 0, unroll=4)


def _attn_layer(qkvs):
    B, S, _ = qkvs[0].shape
    n_pairs = GROUP_WIDTH // LANES
    in_specs, args = [], []
    for arr in qkvs:
        for which in range(3):
            in_specs.append(pl.BlockSpec(
                (1, S, LANES), lambda b, p, which=which: (b, 0, which * n_pairs + p)))
            args.append(arr)
    return pl.pallas_call(
        functools.partial(_attn_kernel, seq=S),
        grid=(B, n_pairs),
        in_specs=in_specs,
        out_specs=pl.BlockSpec((1, S, LANES), lambda b, p: (b, 0, p)),
        out_shape=jax.ShapeDtypeStruct((B, S, GROUP_WIDTH), BF16),
        scratch_shapes=[pltpu.VMEM((N_GROUPS, S, LANES), F32)] * 3
        + [pltpu.VMEM((3, WINDOW_STEPS, 2 * WINDOW_STEPS), F32)],
        compiler_params=_params(),
        name="dilated_attention",
    )(*args)


def _oproj_kernel(h_ref, o_ref, w_ref, gpost_ref, out_ref):
    m = jnp.dot(o_ref[0], w_ref[...], preferred_element_type=F32)
    out_ref[0] = h_ref[0] + _rms(m, gpost_ref[...])


def _oproj_layer(h, o, w, gpost, tm=512):
    B, S, D = h.shape
    return pl.pallas_call(
        _oproj_kernel,
        grid=(B, S // tm),
        in_specs=[_tile_spec(tm, D), _tile_spec(tm, GROUP_WIDTH), _const_spec(w.shape),
                  _const_spec((1, D))],
        out_specs=_tile_spec(tm, D),
        out_shape=jax.ShapeDtypeStruct(h.shape, F32),
        compiler_params=_params(),
        name="attn_out_proj",
    )(h, o, w, gpost)


def _rope_lane_constants():
    inv_freq = ROPE_THETA ** (-jnp.arange(0, ROT_DIM, 2, dtype=F32) / ROT_DIM)
    dim = jnp.arange(LANES) % HEAD_DIM
    half = ROT_DIM // 2
    invf = jnp.where(dim < ROT_DIM, inv_freq[dim % half], 0.0).astype(F32)
    sign = jnp.where(dim < half, -1.0, 1.0).astype(F32)
    return invf[None, :], sign[None, :]


def _prep_qkv_weight(w):
    D = w.shape[0]
    w = w.reshape(D, 3, N_GROUPS, GROUP_WIDTH)
    scale = jnp.array([HEAD_DIM ** -0.5, 1.0, 1.0], F32)[None, :, None, None]
    w = (w * scale).transpose(2, 0, 1, 3).reshape(N_GROUPS, D, 3 * GROUP_WIDTH)
    return w.astype(BF16)


def kernel(x, positions, mix_norm_pre, mix_norm_post, ffn_norm_pre, ffn_norm_post,
           attn_w_qkv, attn_w_o, sc_w_in, sc_w_conv, sc_w_out,
           cc_w_pw1, cc_b_pw1, cc_w_dw, cc_b_dw, cc_ln_g, cc_ln_b, cc_w_pw2, cc_b_pw2,
           ffn_w_in, ffn_w_conv, ffn_w_out):
    B, S, D = x.shape
    row = lambda v: v.reshape(1, -1)

    ffn_win = ffn_w_in.astype(BF16).reshape(DEPTH, D, 2, N_FFN_CHUNKS, FFN_CHUNK).transpose(0, 3, 2, 1, 4)
    ffn_wconv = ffn_w_conv.reshape(DEPTH, 3, 2, N_FFN_CHUNKS, FFN_CHUNK).transpose(0, 3, 2, 1, 4)
    ffn_wconv = jnp.pad(ffn_wconv, ((0, 0), (0, 0), (0, 0), (0, SUBLANES - 3), (0, 0)))
    ffn_wout = ffn_w_out.astype(BF16).reshape(DEPTH, N_FFN_CHUNKS, FFN_CHUNK, D)

    invf_lane, sign_lane = _rope_lane_constants()
    pos_b = jnp.broadcast_to(positions.astype(F32)[:, :, None], (B, S, LANES))
    cos_t, sin_t = _rope_tables(pos_b, invf_lane, sign_lane)

    h = x
    for i in range(DEPTH):
        kind, j = i % N_MIXERS, i // N_MIXERS
        gpre, gpost = row(mix_norm_pre[i]), row(mix_norm_post[i])
        if kind == 0:
            qkvs = _qkv_layer(h, gpre, cos_t, sin_t, _prep_qkv_weight(attn_w_qkv[j]))
            o = _attn_layer(qkvs)
            h = _oproj_layer(h, o, attn_w_o[j].astype(BF16), gpost)
        elif kind == 1:
            h = _sc_layer(h, gpre, gpost, sc_w_in[j].astype(BF16),
                          jnp.pad(sc_w_conv[j], ((0, SUBLANES - 3), (0, 0))),
                          sc_w_out[j].astype(BF16))
        else:
            h = _cc_layer(h, gpre, gpost, cc_w_pw1[j].astype(BF16), row(cc_b_pw1[j]),
                          jnp.pad(cc_w_dw[j], ((0, 1), (0, 0))), row(cc_b_dw[j]),
                          row(cc_ln_g[j]), row(cc_ln_b[j]), cc_w_pw2[j].astype(BF16),
                          row(cc_b_pw2[j]))
        h = _ffn_layer(h, row(ffn_norm_pre[i]), row(ffn_norm_post[i]),
                       ffn_win[i], ffn_wconv[i], ffn_wout[i])
    return h
```

```python
import functools

import jax
import jax.numpy as jnp
from jax import lax
from jax.experimental import pallas as pl
from jax.experimental.pallas import tpu as pltpu

F32 = jnp.float32
BF16 = jnp.bfloat16

D_MODEL = 1024
DEPTH = 4
N_MIXERS = 3
ATTN_GROUPS = ((128, 1), (512, 4), (2048, 16))
DILATIONS = tuple(d for _, d in ATTN_GROUPS)
N_GROUPS = len(ATTN_GROUPS)
HEADS_PER_GROUP = 16
HEAD_DIM = 64
ROT_DIM = HEAD_DIM // 4
ROPE_THETA = 500000.0
GROUP_WIDTH = HEADS_PER_GROUP * HEAD_DIM
WINDOW_STEPS = 128
CONF_CONV_WIDTH = 31
FFN_DIM = 2816
EPS_RMS = 1e-6
EPS_LN = 1e-5
NEG_INF = -1e30

LANES = 128
SUBLANES = 8
MXU_WIDTH = 256
FFN_CHUNK = MXU_WIDTH
N_FFN_CHUNKS = FFN_DIM // FFN_CHUNK
CONF_HALO = 32
CONF_ROW_BLOCK = 32
CONF_COL_BLOCK = 512
VMEM_LIMIT = 60 * 1024 * 1024


def _rms(x, g):
    return x * lax.rsqrt(jnp.mean(x * x, axis=-1, keepdims=True) + EPS_RMS) * g


def _sigmoid(x):
    return 1.0 / (1.0 + jnp.exp(-x))


def _const_spec(shape):
    n = len(shape)
    return pl.BlockSpec(shape, lambda b, s: (0,) * n, pipeline_mode=pl.Buffered(1))


def _tile_spec(tm, width):
    return pl.BlockSpec((1, tm, width), lambda b, s: (b, s, 0))


def _params():
    return pltpu.CompilerParams(
        dimension_semantics=("arbitrary", "arbitrary"), vmem_limit_bytes=VMEM_LIMIT)


def _ffn_kernel(h_ref, gpre_ref, gpost_ref, win_ref, wconv_ref, wout_ref, out_ref,
                xn_ref, ybuf_ref, carry_ref, act_ref, *, sub, n_sub):
    @pl.when(pl.program_id(1) == 0)
    def _():
        carry_ref[...] = jnp.zeros_like(carry_ref)

    n_out_tiles = D_MODEL // MXU_WIDTH
    out_tile_after = [N_FFN_CHUNKS - 1 - 2 * j for j in range(n_out_tiles)][::-1]

    def second_matmul_tile(j, tile):
        cols = slice(tile * MXU_WIDTH, (tile + 1) * MXU_WIDTH)
        f = None
        for k in range(N_FFN_CHUNKS):
            part = jnp.dot(act_ref[j % 2, k], wout_ref[k * FFN_CHUNK:(k + 1) * FFN_CHUNK, cols],
                           preferred_element_type=F32)
            f = part if f is None else f + part
        return f

    def finish(j, f_tiles):
        rows = slice(j * sub, (j + 1) * sub)
        f = jnp.concatenate(f_tiles, axis=1)
        out_ref[0, rows, :] = h_ref[0, rows, :] + _rms(f, gpost_ref[...])

    pending = None
    for j in range(n_sub):
        rows = slice(j * sub, (j + 1) * sub)
        xn_ref[j % 2] = _rms(h_ref[0, rows, :], gpre_ref[...]).astype(BF16)
        for c in range(N_FFN_CHUNKS):
            u = []
            for half in range(2):
                col = half * FFN_DIM + c * FFN_CHUNK
                ybuf = ybuf_ref.at[c % 2, half]
                taps = wconv_ref[:, col:col + FFN_CHUNK]
                y = jnp.dot(xn_ref[j % 2], win_ref[:, col:col + FFN_CHUNK],
                            preferred_element_type=F32)
                ybuf[0:SUBLANES, :] = carry_ref[c, half]
                ybuf[SUBLANES:SUBLANES + sub, :] = y
                carry_ref[c, half] = ybuf[sub:sub + SUBLANES, :]
                u.append(taps[0:1] * ybuf[SUBLANES - 2:SUBLANES - 2 + sub, :]
                         + taps[1:2] * ybuf[SUBLANES - 1:SUBLANES - 1 + sub, :]
                         + taps[2:3] * y)
            act_ref[j % 2, c] = (u[0] * _sigmoid(u[0]) * u[1]).astype(BF16)
            if pending is not None and c in out_tile_after:
                pending[1].append(second_matmul_tile(pending[0], len(pending[1])))
        if pending is not None:
            finish(*pending)
        pending = (j, [])
    for tile in range(n_out_tiles):
        pending[1].append(second_matmul_tile(pending[0], tile))
    finish(*pending)


def _ffn_layer(h, gpre, gpost, win, wconv, wout, tm=1024, sub=512):
    B, S, D = h.shape
    return pl.pallas_call(
        functools.partial(_ffn_kernel, sub=sub, n_sub=tm // sub),
        grid=(B, S // tm),
        in_specs=[_tile_spec(tm, D), _const_spec((1, D)), _const_spec((1, D)),
                  _const_spec(win.shape), _const_spec(wconv.shape), _const_spec(wout.shape)],
        out_specs=_tile_spec(tm, D),
        out_shape=jax.ShapeDtypeStruct(h.shape, F32),
        scratch_shapes=[pltpu.VMEM((2, sub, D), BF16),
                        pltpu.VMEM((2, 2, sub + SUBLANES, FFN_CHUNK), F32),
                        pltpu.VMEM((N_FFN_CHUNKS, 2, SUBLANES, FFN_CHUNK), F32),
                        pltpu.VMEM((2, N_FFN_CHUNKS, sub, FFN_CHUNK), BF16)],
        compiler_params=_params(),
        name="conv_ffn",
    )(h, gpre, gpost, win, wconv, wout)


def _sc_kernel(h_ref, gpre_ref, gpost_ref, win_ref, wconv_ref, wout_ref, o_ref, ubuf_ref, *, tm):
    D = D_MODEL
    x = h_ref[0]
    xn = _rms(x, gpre_ref[...]).astype(BF16)

    @pl.when(pl.program_id(1) == 0)
    def _():
        ubuf_ref[0:SUBLANES, :] = jnp.zeros((SUBLANES, D), F32)

    p = jnp.dot(xn, win_ref[...], preferred_element_type=F32)
    u = p[:, D:2 * D] * p[:, 2 * D:]
    ubuf_ref[SUBLANES:SUBLANES + tm, :] = u
    taps = wconv_ref[...]
    conv = (taps[0:1] * ubuf_ref[SUBLANES - 2:SUBLANES - 2 + tm, :]
            + taps[1:2] * ubuf_ref[SUBLANES - 1:SUBLANES - 1 + tm, :]
            + taps[2:3] * u)
    ubuf_ref[0:SUBLANES, :] = ubuf_ref[tm:tm + SUBLANES, :]
    z = (p[:, :D] * conv).astype(BF16)
    m = jnp.dot(z, wout_ref[...], preferred_element_type=F32)
    o_ref[0] = x + _rms(m, gpost_ref[...])


def _sc_layer(h, gpre, gpost, win, wconv, wout, tm=512):
    B, S, D = h.shape
    return pl.pallas_call(
        functools.partial(_sc_kernel, tm=tm),
        grid=(B, S // tm),
        in_specs=[_tile_spec(tm, D), _const_spec((1, D)), _const_spec((1, D)),
                  _const_spec(win.shape), _const_spec(wconv.shape), _const_spec(wout.shape)],
        out_specs=_tile_spec(tm, D),
        out_shape=jax.ShapeDtypeStruct(h.shape, F32),
        scratch_shapes=[pltpu.VMEM((tm + SUBLANES, D), F32)],
        compiler_params=_params(),
        name="short_conv",
    )(h, gpre, gpost, win, wconv, wout)


def _cc_kernel(h_ref, gpre_ref, gpost_ref, w1_ref, b1_ref, wdw_ref, bdw_ref, lng_ref, lnb_ref,
               w2_ref, b2_ref, o_ref, sh_ref, sw_ref, *, tm):
    D = D_MODEL
    K = CONF_CONV_WIDTH
    ext = tm + CONF_HALO - SUBLANES
    x = h_ref[0]
    xn = _rms(x, gpre_ref[...]).astype(BF16)

    @pl.when(pl.program_id(1) == 0)
    def _():
        sh_ref[0, 0:CONF_HALO, :] = jnp.zeros((CONF_HALO, D), F32)

    p = jnp.dot(xn, w1_ref[...], preferred_element_type=F32) + b1_ref[...]
    sh_ref[0, CONF_HALO:CONF_HALO + tm, :] = p[:, :D] * _sigmoid(p[:, D:])
    for r in range(1, SUBLANES):
        sh_ref[r, 0:ext, :] = sh_ref[0, SUBLANES - r:SUBLANES - r + ext, :]

    def row_block(i, carry):
        base = pl.multiple_of(i * CONF_ROW_BLOCK, CONF_ROW_BLOCK)
        halves = []
        for c0 in range(0, D, CONF_COL_BLOCK):
            cols = slice(c0, c0 + CONF_COL_BLOCK)
            n_sub = CONF_ROW_BLOCK // SUBLANES
            accs = [jnp.zeros((SUBLANES, CONF_COL_BLOCK), F32) + bdw_ref[:, cols]] * n_sub
            for k in range(K):
                shift = K - 1 - k
                q, r = divmod(shift, SUBLANES)
                off = CONF_HALO - SUBLANES * q if r == 0 else CONF_HALO - SUBLANES - SUBLANES * q
                tap = wdw_ref[k, :, cols]
                xs = sh_ref[r, pl.ds(base + off, CONF_ROW_BLOCK), cols]
                accs = [a + tap * xs[j * SUBLANES:(j + 1) * SUBLANES]
                        for j, a in enumerate(accs)]
            halves.append(jnp.concatenate(accs, axis=0))
        acc = jnp.concatenate(halves, axis=1)
        mu = jnp.mean(acc, axis=-1, keepdims=True)
        cen = acc - mu
        var = jnp.mean(cen * cen, axis=-1, keepdims=True)
        ln = cen * lax.rsqrt(var + EPS_LN) * lng_ref[...] + lnb_ref[...]
        sw_ref[pl.ds(base, CONF_ROW_BLOCK), :] = (ln * _sigmoid(ln)).astype(BF16)
        return carry

    lax.fori_loop(0, tm // CONF_ROW_BLOCK, row_block, 0, unroll=2)
    sh_ref[0, 0:CONF_HALO, :] = sh_ref[0, tm:tm + CONF_HALO, :]
    m = jnp.dot(sw_ref[...], w2_ref[...], preferred_element_type=F32) + b2_ref[...]
    o_ref[0] = x + _rms(m, gpost_ref[...])


def _cc_layer(h, gpre, gpost, w1, b1, wdw, bdw, lng, lnb, w2, b2, tm=256):
    B, S, D = h.shape
    vec = _const_spec((1, D))
    return pl.pallas_call(
        functools.partial(_cc_kernel, tm=tm),
        grid=(B, S // tm),
        in_specs=[_tile_spec(tm, D), vec, vec, _const_spec(w1.shape), _const_spec(b1.shape),
                  _const_spec(wdw.shape), vec, vec, vec, _const_spec(w2.shape), vec],
        out_specs=_tile_spec(tm, D),
        out_shape=jax.ShapeDtypeStruct(h.shape, F32),
        scratch_shapes=[pltpu.VMEM((SUBLANES, tm + CONF_HALO, D), F32),
                        pltpu.VMEM((tm, D), BF16)],
        compiler_params=_params(),
        name="conformer_conv",
    )(h, gpre, gpost, w1, b1, wdw, bdw, lng, lnb, w2, b2)


def _rope_table_kernel(pos_ref, invf_ref, sign_ref, cos_ref, sin_ref):
    ang = pos_ref[0] * invf_ref[...]
    cos_ref[0] = jnp.cos(ang)
    sin_ref[0] = jnp.sin(ang) * sign_ref[...]


def _rope_tables(pos_b, invf_lane, sign_lane, tm=512):
    B, S, _ = pos_b.shape
    out = jax.ShapeDtypeStruct(pos_b.shape, F32)
    return pl.pallas_call(
        _rope_table_kernel,
        grid=(B, S // tm),
        in_specs=[_tile_spec(tm, LANES), _const_spec((1, LANES)), _const_spec((1, LANES))],
        out_specs=[_tile_spec(tm, LANES), _tile_spec(tm, LANES)],
        out_shape=[out, out],
        compiler_params=_params(),
        name="rope_tables",
    )(pos_b, invf_lane, sign_lane)


def _qkv_kernel(h_ref, g_ref, cos_ref, sin_ref, w_ref, o0_ref, o1_ref, o2_ref,
                a_ref, ap_ref, cp_ref, sp_ref, *, tm):
    n_lane_blocks = D_MODEL // LANES
    a_nat = _rms(h_ref[0], g_ref[...])
    for j in range(n_lane_blocks):
        a_ref[j] = a_nat[:, j * LANES:(j + 1) * LANES]
    lane = lax.broadcasted_iota(jnp.int32, (1, LANES), 1)
    first_half = (lane % HEAD_DIM) < (ROT_DIM // 2)
    for g, (d, o_ref) in enumerate(zip(DILATIONS, (o0_ref, o1_ref, o2_ref))):
        rows = tm // d
        if d == 1:
            a = a_nat.astype(BF16)
            cos, sin = cos_ref[0], sin_ref[0]
        else:
            for r in range(d):
                dst = slice(r * rows, (r + 1) * rows)
                for j in range(n_lane_blocks):
                    ap_ref[dst, j * LANES:(j + 1) * LANES] = (
                        a_ref[j, pl.ds(r, rows, stride=d), :].astype(BF16))
                cp_ref[dst, :] = cos_ref[0, pl.ds(r, rows, stride=d), :]
                sp_ref[dst, :] = sin_ref[0, pl.ds(r, rows, stride=d), :]
            a = ap_ref[...]
            cos, sin = cp_ref[...], sp_ref[...]
        blocks_per_part = GROUP_WIDTH // LANES
        parts = []
        for which in range(3):
            c0 = (which * N_GROUPS + g) * GROUP_WIDTH
            parts.append(jnp.dot(a, w_ref[:, c0:c0 + GROUP_WIDTH], preferred_element_type=F32))
        blocks = []
        for j in range(3 * blocks_per_part):
            jj = j % blocks_per_part
            blk = parts[j // blocks_per_part][:, jj * LANES:(jj + 1) * LANES]
            if j < 2 * blocks_per_part:
                partner = jnp.where(first_half,
                                    pltpu.roll(blk, LANES - ROT_DIM // 2, 1),
                                    pltpu.roll(blk, ROT_DIM // 2, 1))
                blk = blk * cos + partner * sin
            blocks.append(blk.astype(BF16))
        out = jnp.concatenate(blocks, axis=1)
        for r in range(d):
            o_ref[0, r] = out[r * rows:(r + 1) * rows, :]


def _qkv_layer(h, g, cos_t, sin_t, w, tm=256):
    B, S, D = h.shape
    W3 = 3 * GROUP_WIDTH
    out_shapes = [jax.ShapeDtypeStruct((B, d, S // d, W3), BF16) for d in DILATIONS]
    out_specs = [pl.BlockSpec((1, d, tm // d, W3), lambda b, s: (b, 0, s, 0)) for d in DILATIONS]
    outs = pl.pallas_call(
        functools.partial(_qkv_kernel, tm=tm),
        grid=(B, S // tm),
        in_specs=[_tile_spec(tm, D), _const_spec((1, D)), _tile_spec(tm, LANES),
                  _tile_spec(tm, LANES), _const_spec(w.shape)],
        out_specs=out_specs,
        out_shape=out_shapes,
        scratch_shapes=[pltpu.VMEM((D // LANES, tm, LANES), F32), pltpu.VMEM((tm, D), BF16),
                        pltpu.VMEM((tm, LANES), F32), pltpu.VMEM((tm, LANES), F32)],
        compiler_params=_params(),
        name="qkv_rope",
    )(h, g, cos_t, sin_t, w)
    return [o.reshape(B, S, W3) for o in outs]


def _attn_kernel(*refs, seq):
    qkv_refs = refs[:3 * N_GROUPS]
    o_ref = refs[3 * N_GROUPS]
    acc_ref, m_ref, l_ref, bias_ref = refs[3 * N_GROUPS + 1:]
    W = WINDOW_STEPS
    lane = lax.broadcasted_iota(jnp.int32, (1, LANES), 1)
    head0 = lane < HEAD_DIM

    @pl.when((pl.program_id(0) == 0) & (pl.program_id(1) == 0))
    def _():
        qi = lax.broadcasted_iota(jnp.int32, (W, 2 * W), 0)
        kj = lax.broadcasted_iota(jnp.int32, (W, 2 * W), 1)
        dist = qi - kj + W
        in_window = (dist >= 0) & (dist <= W)
        for t, valid in enumerate((in_window, in_window & (kj >= W), kj <= qi)):
            bias_ref[t] = jnp.where(valid, 0.0, NEG_INF).astype(F32)

    for g, d in enumerate(DILATIONS):
        q_ref, k_ref, v_ref = qkv_refs[3 * g:3 * g + 3]
        blocks_per_sub = seq // d // W
        n_keys = W if blocks_per_sub == 1 else 2 * W

        def q_block(i, carry, q_ref=q_ref, k_ref=k_ref, v_ref=v_ref, d=d, g=g,
                    blocks_per_sub=blocks_per_sub, n_keys=n_keys):
            base = pl.multiple_of(i * W, W)
            r = i // blocks_per_sub
            n = i % blocks_per_sub
            if n_keys == W:
                kstart = base
                bias = bias_ref[2, :, 0:W]
            else:
                kstart = pl.multiple_of(jnp.maximum(base - W, 0), W)
                bias = bias_ref[jnp.where(i == 0, 2, jnp.where(n == 0, 1, 0))]
            q = q_ref[0, pl.ds(base, W), :]
            kk = k_ref[0, pl.ds(kstart, n_keys), :]
            vv = v_ref[0, pl.ds(kstart, n_keys), :]
            zero = jnp.zeros_like(q)
            q2 = jnp.concatenate([jnp.where(head0, q, zero), jnp.where(head0, zero, q)], axis=0)
            s2 = lax.dot_general(q2, kk, (((1,), (1,)), ((), ())), preferred_element_type=F32)
            ps, ms, ls = [], [], []
            for hd in range(2):
                s = s2[hd * W:(hd + 1) * W] + bias
                m = jnp.max(s, axis=1, keepdims=True)
                p = jnp.exp(s - m)
                ls.append(jnp.sum(p, axis=1, keepdims=True))
                ms.append(m)
                ps.append(p.astype(BF16))
            o2 = jnp.dot(jnp.concatenate(ps, axis=0), vv, preferred_element_type=F32)
            t0 = n * (W * d) + r
            dst = pl.ds(t0, W, stride=d) if d > 1 else pl.ds(t0, W)
            acc_ref[g, dst, :] = jnp.where(head0, o2[:W], o2[W:])
            m_ref[g, dst, :] = jnp.where(head0, ms[0], ms[1])
            l_ref[g, dst, :] = jnp.where(head0, ls[0], ls[1])
            return carry

        lax.fori_loop(0, seq // W, q_block, 0, unroll=16)

    def merge(i, carry):
        rows = pl.ds(pl.multiple_of(i * W, W), W)
        m_all = [m_ref[g, rows, :] for g in range(N_GROUPS)]
        m_max = functools.reduce(jnp.maximum, m_all)
        num = jnp.zeros((W, LANES), F32)
        den = jnp.zeros((W, LANES), F32)
        for g in range(N_GROUPS):
            wgt = jnp.exp(m_all[g] - m_max)
            num = num + wgt * acc_ref[g, rows, :]
            den = den + wgt * l_ref[g, rows, :]
        o_ref[0, rows, :] = (num / den).astype(BF16)
        return carry

    lax.fori_loop(0, seq // W, merge, 0, unroll=4)


def _attn_layer(qkvs):
    B, S, _ = qkvs[0].shape
    n_pairs = GROUP_WIDTH // LANES
    in_specs, args = [], []
    for arr in qkvs:
        for which in range(3):
            in_specs.append(pl.BlockSpec(
                (1, S, LANES), lambda b, p, which=which: (b, 0, which * n_pairs + p)))
            args.append(arr)
    return pl.pallas_call(
        functools.partial(_attn_kernel, seq=S),
        grid=(B, n_pairs),
        in_specs=in_specs,
        out_specs=pl.BlockSpec((1, S, LANES), lambda b, p: (b, 0, p)),
        out_shape=jax.ShapeDtypeStruct((B, S, GROUP_WIDTH), BF16),
        scratch_shapes=[pltpu.VMEM((N_GROUPS, S, LANES), F32)] * 3
        + [pltpu.VMEM((3, WINDOW_STEPS, 2 * WINDOW_STEPS), F32)],
        compiler_params=_params(),
        name="dilated_attention",
    )(*args)


def _oproj_kernel(h_ref, o_ref, w_ref, gpost_ref, out_ref):
    m = jnp.dot(o_ref[0], w_ref[...], preferred_element_type=F32)
    out_ref[0] = h_ref[0] + _rms(m, gpost_ref[...])


def _oproj_layer(h, o, w, gpost, tm=512):
    B, S, D = h.shape
    return pl.pallas_call(
        _oproj_kernel,
        grid=(B, S // tm),
        in_specs=[_tile_spec(tm, D), _tile_spec(tm, GROUP_WIDTH), _const_spec(w.shape),
                  _const_spec((1, D))],
        out_specs=_tile_spec(tm, D),
        out_shape=jax.ShapeDtypeStruct(h.shape, F32),
        compiler_params=_params(),
        name="attn_out_proj",
    )(h, o, w, gpost)


def _rope_lane_constants():
    inv_freq = ROPE_THETA ** (-jnp.arange(0, ROT_DIM, 2, dtype=F32) / ROT_DIM)
    dim = jnp.arange(LANES) % HEAD_DIM
    half = ROT_DIM // 2
    invf = jnp.where(dim < ROT_DIM, inv_freq[dim % half], 0.0).astype(F32)
    sign = jnp.where(dim < half, -1.0, 1.0).astype(F32)
    return invf[None, :], sign[None, :]


def _prep_qkv_weight(w):
    q_cols = jnp.arange(w.shape[1]) < N_GROUPS * GROUP_WIDTH
    scale = jnp.where(q_cols, HEAD_DIM ** -0.5, 1.0).astype(F32)
    return (w * scale[None, :]).astype(BF16)


def kernel(x, positions, mix_norm_pre, mix_norm_post, ffn_norm_pre, ffn_norm_post,
           attn_w_qkv, attn_w_o, sc_w_in, sc_w_conv, sc_w_out,
           cc_w_pw1, cc_b_pw1, cc_w_dw, cc_b_dw, cc_ln_g, cc_ln_b, cc_w_pw2, cc_b_pw2,
           ffn_w_in, ffn_w_conv, ffn_w_out):
    B, S, D = x.shape
    row = lambda v: v.reshape(1, -1)

    ffn_win = ffn_w_in.astype(BF16)
    ffn_wconv = jnp.pad(ffn_w_conv, ((0, 0), (0, SUBLANES - 3), (0, 0)))
    ffn_wout = ffn_w_out.astype(BF16)

    invf_lane, sign_lane = _rope_lane_constants()
    pos_b = jnp.broadcast_to(positions.astype(F32)[:, :, None], (B, S, LANES))
    cos_t, sin_t = _rope_tables(pos_b, invf_lane, sign_lane)

    h = x
    for i in range(DEPTH):
        kind, j = i % N_MIXERS, i // N_MIXERS
        gpre, gpost = row(mix_norm_pre[i]), row(mix_norm_post[i])
        if kind == 0:
            qkvs = _qkv_layer(h, gpre, cos_t, sin_t, _prep_qkv_weight(attn_w_qkv[j]))
            h = _oproj_layer(h, _attn_layer(qkvs), attn_w_o[j].astype(BF16), gpost)
        elif kind == 1:
            h = _sc_layer(h, gpre, gpost, sc_w_in[j].astype(BF16),
                          jnp.pad(sc_w_conv[j], ((0, SUBLANES - 3), (0, 0))),
                          sc_w_out[j].astype(BF16))
        else:
            h = _cc_layer(h, gpre, gpost, cc_w_pw1[j].astype(BF16), row(cc_b_pw1[j]),
                          jnp.broadcast_to(cc_w_dw[j][:, None, :],
                                           (CONF_CONV_WIDTH, SUBLANES, D)),
                          row(cc_b_dw[j]), row(cc_ln_g[j]), row(cc_ln_b[j]),
                          cc_w_pw2[j].astype(BF16), row(cc_b_pw2[j]))
        h = _ffn_layer(h, row(ffn_norm_pre[i]), row(ffn_norm_post[i]),
                       ffn_win[i], ffn_wconv[i], ffn_wout[i])
    return h
```

```python
import functools

import jax
import jax.numpy as jnp
from jax import lax
from jax.experimental import pallas as pl
from jax.experimental.pallas import tpu as pltpu

F32 = jnp.float32
BF16 = jnp.bfloat16

D_MODEL = 1024
DEPTH = 4
N_MIXERS = 3
ATTN_GROUPS = ((128, 1), (512, 4), (2048, 16))
DILATIONS = tuple(d for _, d in ATTN_GROUPS)
N_GROUPS = len(ATTN_GROUPS)
HEADS_PER_GROUP = 16
HEAD_DIM = 64
ROT_DIM = HEAD_DIM // 4
ROPE_THETA = 500000.0
GROUP_WIDTH = HEADS_PER_GROUP * HEAD_DIM
WINDOW_STEPS = 128
CONF_CONV_WIDTH = 31
FFN_DIM = 2816
EPS_RMS = 1e-6
EPS_LN = 1e-5
NEG_INF = -1e30

LANES = 128
SUBLANES = 8
MXU_WIDTH = 256
FFN_CHUNK = MXU_WIDTH
N_FFN_CHUNKS = FFN_DIM // FFN_CHUNK
CONF_ROW_GROUPS = 4
CONF_COL_BLOCK = 512
VMEM_LIMIT = 60 * 1024 * 1024


def _rms(x, g):
    return x * lax.rsqrt(jnp.mean(x * x, axis=-1, keepdims=True) + EPS_RMS) * g


def _sigmoid(x):
    return 1.0 / (1.0 + jnp.exp(-x))


def _const_spec(shape):
    n = len(shape)
    return pl.BlockSpec(shape, lambda b, s: (0,) * n, pipeline_mode=pl.Buffered(1))


def _tile_spec(tm, width):
    return pl.BlockSpec((1, tm, width), lambda b, s: (b, s, 0))


def _params():
    return pltpu.CompilerParams(
        dimension_semantics=("arbitrary", "arbitrary"), vmem_limit_bytes=VMEM_LIMIT)


def _interleave(x, scratch_ref):
    p = x.shape[0] // SUBLANES
    n_lane_blocks = x.shape[1] // LANES
    for s in range(SUBLANES):
        for lb in range(n_lane_blocks):
            scratch_ref[lb, pl.ds(s, p, stride=SUBLANES), :] = (
                x[s * p:(s + 1) * p, lb * LANES:(lb + 1) * LANES])
    return jnp.concatenate([scratch_ref[lb] for lb in range(n_lane_blocks)], axis=1)


def _deinterleave(y, scratch_ref):
    p = y.shape[0] // SUBLANES
    n_lane_blocks = y.shape[1] // LANES
    for lb in range(n_lane_blocks):
        scratch_ref[lb] = y[:, lb * LANES:(lb + 1) * LANES]
    return jnp.concatenate(
        [jnp.concatenate([scratch_ref[lb, pl.ds(s, p, stride=SUBLANES), :]
                          for lb in range(n_lane_blocks)], axis=1)
         for s in range(SUBLANES)], axis=0)


def _history(prev_group, cur_group):
    first_sublane = lax.broadcasted_iota(jnp.int32, cur_group.shape, 0) == 0
    return jnp.where(first_sublane, pltpu.roll(prev_group, 1, 0), pltpu.roll(cur_group, 1, 0))


def _ffn_kernel(h_ref, gpre_ref, gpost_ref, win_ref, wconv_ref, wout_ref, out_ref,
                xn_ref, ybuf_ref, carry_ref, act_ref, perm_ref, unperm_ref, *, sub, n_sub):
    @pl.when(pl.program_id(1) == 0)
    def _():
        carry_ref[...] = jnp.zeros_like(carry_ref)

    n_out_tiles = D_MODEL // MXU_WIDTH
    out_tile_after = [N_FFN_CHUNKS - 1 - 2 * j for j in range(n_out_tiles)][::-1]
    interleave = functools.partial(_interleave, scratch_ref=perm_ref)
    deinterleave = functools.partial(_deinterleave, scratch_ref=unperm_ref)
    history = _history

    def second_matmul_tile(j, tile):
        cols = slice(tile * MXU_WIDTH, (tile + 1) * MXU_WIDTH)
        f = None
        for k in range(N_FFN_CHUNKS):
            part = jnp.dot(act_ref[j % 2, k], wout_ref[k * FFN_CHUNK:(k + 1) * FFN_CHUNK, cols],
                           preferred_element_type=F32)
            f = part if f is None else f + part
        return f

    def finish(j, f_tiles):
        rows = slice(j * sub, (j + 1) * sub)
        f = jnp.concatenate(f_tiles, axis=1)
        out_ref[0, rows, :] = h_ref[0, rows, :] + deinterleave(_rms(f, gpost_ref[...]))

    G = SUBLANES
    pending = None
    for j in range(n_sub):
        rows = slice(j * sub, (j + 1) * sub)
        xn_ref[j % 2] = interleave(_rms(h_ref[0, rows, :], gpre_ref[...])).astype(BF16)
        for c in range(N_FFN_CHUNKS):
            u = []
            for half in range(2):
                col = half * FFN_DIM + c * FFN_CHUNK
                ybuf = ybuf_ref.at[c % 2, half]
                taps = wconv_ref[:, col:col + FFN_CHUNK]
                y = jnp.dot(xn_ref[j % 2], win_ref[:, col:col + FFN_CHUNK],
                            preferred_element_type=F32)
                prev = carry_ref[c, half]
                last = y[sub - 2 * G:, :]
                ybuf[0:G, :] = history(prev[0:G], last[0:G])
                ybuf[G:2 * G, :] = history(prev[G:], last[G:])
                ybuf[2 * G:2 * G + sub, :] = y
                carry_ref[c, half] = last
                u.append(taps[0:1] * ybuf[0:sub, :] + taps[1:2] * ybuf[G:G + sub, :]
                         + taps[2:3] * y)
            act_ref[j % 2, c] = (u[0] * _sigmoid(u[0]) * u[1]).astype(BF16)
            if pending is not None and c in out_tile_after:
                pending[1].append(second_matmul_tile(pending[0], len(pending[1])))
        if pending is not None:
            finish(*pending)
        pending = (j, [])
    for tile in range(n_out_tiles):
        pending[1].append(second_matmul_tile(pending[0], tile))
    finish(*pending)


def _ffn_layer(h, gpre, gpost, win, wconv, wout, tm=1024, sub=512):
    B, S, D = h.shape
    return pl.pallas_call(
        functools.partial(_ffn_kernel, sub=sub, n_sub=tm // sub),
        grid=(B, S // tm),
        in_specs=[_tile_spec(tm, D), _const_spec((1, D)), _const_spec((1, D)),
                  _const_spec(win.shape), _const_spec(wconv.shape), _const_spec(wout.shape)],
        out_specs=_tile_spec(tm, D),
        out_shape=jax.ShapeDtypeStruct(h.shape, F32),
        scratch_shapes=[pltpu.VMEM((2, sub, D), BF16),
                        pltpu.VMEM((2, 2, sub + 2 * SUBLANES, FFN_CHUNK), F32),
                        pltpu.VMEM((N_FFN_CHUNKS, 2, 2 * SUBLANES, FFN_CHUNK), F32),
                        pltpu.VMEM((2, N_FFN_CHUNKS, sub, FFN_CHUNK), BF16),
                        pltpu.VMEM((D // LANES, sub, LANES), F32),
                        pltpu.VMEM((D // LANES, sub, LANES), F32)],
        compiler_params=_params(),
        name="conv_ffn",
    )(h, gpre, gpost, win, wconv, wout)


def _sc_kernel(h_ref, gpre_ref, gpost_ref, win_ref, wconv_ref, wout_ref, o_ref, ubuf_ref, *, tm):
    D = D_MODEL
    x = h_ref[0]
    xn = _rms(x, gpre_ref[...]).astype(BF16)

    @pl.when(pl.program_id(1) == 0)
    def _():
        ubuf_ref[0:SUBLANES, :] = jnp.zeros((SUBLANES, D), F32)

    p = jnp.dot(xn, win_ref[...], preferred_element_type=F32)
    u = p[:, D:2 * D] * p[:, 2 * D:]
    ubuf_ref[SUBLANES:SUBLANES + tm, :] = u
    taps = wconv_ref[...]
    conv = (taps[0:1] * ubuf_ref[SUBLANES - 2:SUBLANES - 2 + tm, :]
            + taps[1:2] * ubuf_ref[SUBLANES - 1:SUBLANES - 1 + tm, :]
            + taps[2:3] * u)
    ubuf_ref[0:SUBLANES, :] = ubuf_ref[tm:tm + SUBLANES, :]
    z = (p[:, :D] * conv).astype(BF16)
    m = jnp.dot(z, wout_ref[...], preferred_element_type=F32)
    o_ref[0] = x + _rms(m, gpost_ref[...])


def _sc_layer(h, gpre, gpost, win, wconv, wout, tm=512):
    B, S, D = h.shape
    return pl.pallas_call(
        functools.partial(_sc_kernel, tm=tm),
        grid=(B, S // tm),
        in_specs=[_tile_spec(tm, D), _const_spec((1, D)), _const_spec((1, D)),
                  _const_spec(win.shape), _const_spec(wconv.shape), _const_spec(wout.shape)],
        out_specs=_tile_spec(tm, D),
        out_shape=jax.ShapeDtypeStruct(h.shape, F32),
        scratch_shapes=[pltpu.VMEM((tm + SUBLANES, D), F32)],
        compiler_params=_params(),
        name="short_conv",
    )(h, gpre, gpost, win, wconv, wout)


def _cc_kernel(h_ref, gpre_ref, gpost_ref, w1_ref, b1_ref, wdw_ref, bdw_ref, lng_ref, lnb_ref,
               w2_ref, b2_ref, o_ref, e_ref, sw_ref, perm_ref, unperm_ref, *, tm):
    D = D_MODEL
    K = CONF_CONV_WIDTH
    P = tm // SUBLANES
    G = CONF_ROW_GROUPS
    x = h_ref[0]
    xn = _interleave(_rms(x, gpre_ref[...]), perm_ref).astype(BF16)

    @pl.when(pl.program_id(1) == 0)
    def _():
        e_ref[P:2 * P] = jnp.zeros((P, SUBLANES, D), F32)

    p = jnp.dot(xn, w1_ref[...], preferred_element_type=F32) + b1_ref[...]
    glu = p[:, :D] * _sigmoid(p[:, D:])
    for j in range(P):
        cur = glu[j * SUBLANES:(j + 1) * SUBLANES]
        e_ref[j] = _history(e_ref[P + j], cur)
        e_ref[P + j] = cur

    def row_block(i, carry):
        g0 = i * G
        blocks = []
        for c0 in range(0, D, CONF_COL_BLOCK):
            cols = slice(c0, c0 + CONF_COL_BLOCK)
            accs = [jnp.zeros((SUBLANES, CONF_COL_BLOCK), F32) + bdw_ref[:, cols]] * G
            for k in range(K):
                tap = wdw_ref[k, :, cols]
                xs = e_ref[pl.ds(P + g0 - (K - 1) + k, G), :, cols]
                accs = [a + tap * xs[jj] for jj, a in enumerate(accs)]
            blocks.append(jnp.concatenate(accs, axis=0))
        acc = jnp.concatenate(blocks, axis=1)
        mu = jnp.mean(acc, axis=-1, keepdims=True)
        cen = acc - mu
        var = jnp.mean(cen * cen, axis=-1, keepdims=True)
        ln = cen * lax.rsqrt(var + EPS_LN) * lng_ref[...] + lnb_ref[...]
        rows = pl.ds(pl.multiple_of(g0 * SUBLANES, G * SUBLANES), G * SUBLANES)
        sw_ref[rows, :] = (ln * _sigmoid(ln)).astype(BF16)
        return carry

    lax.fori_loop(0, P // G, row_block, 0, unroll=2)
    m = jnp.dot(sw_ref[...], w2_ref[...], preferred_element_type=F32) + b2_ref[...]
    o_ref[0] = x + _deinterleave(_rms(m, gpost_ref[...]), unperm_ref)


def _cc_layer(h, gpre, gpost, w1, b1, wdw, bdw, lng, lnb, w2, b2, tm=256):
    B, S, D = h.shape
    vec = _const_spec((1, D))
    return pl.pallas_call(
        functools.partial(_cc_kernel, tm=tm),
        grid=(B, S // tm),
        in_specs=[_tile_spec(tm, D), vec, vec, _const_spec(w1.shape), _const_spec(b1.shape),
                  _const_spec(wdw.shape), vec, vec, vec, _const_spec(w2.shape), vec],
        out_specs=_tile_spec(tm, D),
        out_shape=jax.ShapeDtypeStruct(h.shape, F32),
        scratch_shapes=[pltpu.VMEM((2 * tm // SUBLANES, SUBLANES, D), F32),
                        pltpu.VMEM((tm, D), BF16),
                        pltpu.VMEM((D // LANES, tm, LANES), F32),
                        pltpu.VMEM((D // LANES, tm, LANES), F32)],
        compiler_params=_params(),
        name="conformer_conv",
    )(h, gpre, gpost, w1, b1, wdw, bdw, lng, lnb, w2, b2)


def _rope_table_kernel(pos_ref, invf_ref, sign_ref, cos_ref, sin_ref):
    ang = pos_ref[0] * invf_ref[...]
    cos_ref[0] = jnp.cos(ang)
    sin_ref[0] = jnp.sin(ang) * sign_ref[...]


def _rope_tables(pos_b, invf_lane, sign_lane, tm=512):
    B, S, _ = pos_b.shape
    out = jax.ShapeDtypeStruct(pos_b.shape, F32)
    return pl.pallas_call(
        _rope_table_kernel,
        grid=(B, S // tm),
        in_specs=[_tile_spec(tm, LANES), _const_spec((1, LANES)), _const_spec((1, LANES))],
        out_specs=[_tile_spec(tm, LANES), _tile_spec(tm, LANES)],
        out_shape=[out, out],
        compiler_params=_params(),
        name="rope_tables",
    )(pos_b, invf_lane, sign_lane)


def _qkv_kernel(h_ref, g_ref, cos_ref, sin_ref, w_ref, o0_ref, o1_ref, o2_ref,
                a_ref, ap_ref, cp_ref, sp_ref, *, tm):
    n_lane_blocks = D_MODEL // LANES
    a_nat = _rms(h_ref[0], g_ref[...])
    for j in range(n_lane_blocks):
        a_ref[j] = a_nat[:, j * LANES:(j + 1) * LANES]
    lane = lax.broadcasted_iota(jnp.int32, (1, LANES), 1)
    first_half = (lane % HEAD_DIM) < (ROT_DIM // 2)
    for g, (d, o_ref) in enumerate(zip(DILATIONS, (o0_ref, o1_ref, o2_ref))):
        rows = tm // d
        if d == 1:
            a = a_nat.astype(BF16)
            cos, sin = cos_ref[0], sin_ref[0]
        else:
            for r in range(d):
                dst = slice(r * rows, (r + 1) * rows)
                for j in range(n_lane_blocks):
                    ap_ref[dst, j * LANES:(j + 1) * LANES] = (
                        a_ref[j, pl.ds(r, rows, stride=d), :].astype(BF16))
                cp_ref[dst, :] = cos_ref[0, pl.ds(r, rows, stride=d), :]
                sp_ref[dst, :] = sin_ref[0, pl.ds(r, rows, stride=d), :]
            a = ap_ref[...]
            cos, sin = cp_ref[...], sp_ref[...]
        blocks_per_part = GROUP_WIDTH // LANES
        parts = []
        for which in range(3):
            c0 = (which * N_GROUPS + g) * GROUP_WIDTH
            parts.append(jnp.dot(a, w_ref[:, c0:c0 + GROUP_WIDTH], preferred_element_type=F32))
        blocks = []
        for j in range(3 * blocks_per_part):
            jj = j % blocks_per_part
            blk = parts[j // blocks_per_part][:, jj * LANES:(jj + 1) * LANES]
            if j < 2 * blocks_per_part:
                partner = jnp.where(first_half,
                                    pltpu.roll(blk, LANES - ROT_DIM // 2, 1),
                                    pltpu.roll(blk, ROT_DIM // 2, 1))
                blk = blk * cos + partner * sin
            blocks.append(blk.astype(BF16))
        out = jnp.concatenate(blocks, axis=1)
        for r in range(d):
            o_ref[0, r] = out[r * rows:(r + 1) * rows, :]


def _qkv_layer(h, g, cos_t, sin_t, w, tm=256):
    B, S, D = h.shape
    W3 = 3 * GROUP_WIDTH
    out_shapes = [jax.ShapeDtypeStruct((B, d, S // d, W3), BF16) for d in DILATIONS]
    out_specs = [pl.BlockSpec((1, d, tm // d, W3), lambda b, s: (b, 0, s, 0)) for d in DILATIONS]
    outs = pl.pallas_call(
        functools.partial(_qkv_kernel, tm=tm),
        grid=(B, S // tm),
        in_specs=[_tile_spec(tm, D), _const_spec((1, D)), _tile_spec(tm, LANES),
                  _tile_spec(tm, LANES), _const_spec(w.shape)],
        out_specs=out_specs,
        out_shape=out_shapes,
        scratch_shapes=[pltpu.VMEM((D // LANES, tm, LANES), F32), pltpu.VMEM((tm, D), BF16),
                        pltpu.VMEM((tm, LANES), F32), pltpu.VMEM((tm, LANES), F32)],
        compiler_params=_params(),
        name="qkv_rope",
    )(h, g, cos_t, sin_t, w)
    return [o.reshape(B, S, W3) for o in outs]


def _attn_kernel(*refs, seq):
    qkv_refs = refs[:3 * N_GROUPS]
    o_ref = refs[3 * N_GROUPS]
    acc_ref, m_ref, l_ref, bias_ref = refs[3 * N_GROUPS + 1:]
    W = WINDOW_STEPS
    lane = lax.broadcasted_iota(jnp.int32, (1, LANES), 1)
    head0 = lane < HEAD_DIM

    @pl.when((pl.program_id(0) == 0) & (pl.program_id(1) == 0))
    def _():
        qi = lax.broadcasted_iota(jnp.int32, (W, 2 * W), 0)
        kj = lax.broadcasted_iota(jnp.int32, (W, 2 * W), 1)
        dist = qi - kj + W
        in_window = (dist >= 0) & (dist <= W)
        for t, valid in enumerate((in_window, in_window & (kj >= W), kj <= qi)):
            bias_ref[t] = jnp.where(valid, 0.0, NEG_INF).astype(F32)

    for g, d in enumerate(DILATIONS):
        q_ref, k_ref, v_ref = qkv_refs[3 * g:3 * g + 3]
        blocks_per_sub = seq // d // W
        n_keys = W if blocks_per_sub == 1 else 2 * W

        def q_block(i, carry, q_ref=q_ref, k_ref=k_ref, v_ref=v_ref, d=d, g=g,
                    blocks_per_sub=blocks_per_sub, n_keys=n_keys):
            base = pl.multiple_of(i * W, W)
            r = i // blocks_per_sub
            n = i % blocks_per_sub
            if n_keys == W:
                kstart = base
                bias = bias_ref[2, :, 0:W]
            else:
                kstart = pl.multiple_of(jnp.maximum(base - W, 0), W)
                bias = bias_ref[jnp.where(i == 0, 2, jnp.where(n == 0, 1, 0))]
            q = q_ref[0, pl.ds(base, W), :]
            kk = k_ref[0, pl.ds(kstart, n_keys), :]
            vv = v_ref[0, pl.ds(kstart, n_keys), :]
            zero = jnp.zeros_like(q)
            q2 = jnp.concatenate([jnp.where(head0, q, zero), jnp.where(head0, zero, q)], axis=0)
            s2 = lax.dot_general(q2, kk, (((1,), (1,)), ((), ())), preferred_element_type=F32)
            ps, ms, ls = [], [], []
            for hd in range(2):
                s = s2[hd * W:(hd + 1) * W] + bias
                m = jnp.max(s, axis=1, keepdims=True)
                p = jnp.exp(s - m)
                ls.append(jnp.sum(p, axis=1, keepdims=True))
                ms.append(m)
                ps.append(p.astype(BF16))
            o2 = jnp.dot(jnp.concatenate(ps, axis=0), vv, preferred_element_type=F32)
            t0 = n * (W * d) + r
            dst = pl.ds(t0, W, stride=d) if d > 1 else pl.ds(t0, W)
            acc_ref[g, dst, :] = jnp.where(head0, o2[:W], o2[W:])
            m_ref[g, dst, :] = jnp.where(head0, ms[0], ms[1])
            l_ref[g, dst, :] = jnp.where(head0, ls[0], ls[1])
            return carry

        lax.fori_loop(0, seq // W, q_block, 0, unroll=16)

    def merge(i, carry):
        rows = pl.ds(pl.multiple_of(i * W, W), W)
        m_all = [m_ref[g, rows, :] for g in range(N_GROUPS)]
        m_max = functools.reduce(jnp.maximum, m_all)
        num = jnp.zeros((W, LANES), F32)
        den = jnp.zeros((W, LANES), F32)
        for g in range(N_GROUPS):
            wgt = jnp.exp(m_all[g] - m_max)
            num = num + wgt * acc_ref[g, rows, :]
            den = den + wgt * l_ref[g, rows, :]
        o_ref[0, rows, :] = (num / den).astype(BF16)
        return carry

    lax.fori_loop(0, seq // W, merge, 0, unroll=4)


def _attn_layer(qkvs):
    B, S, _ = qkvs[0].shape
    n_pairs = GROUP_WIDTH // LANES
    in_specs, args = [], []
    for arr in qkvs:
        for which in range(3):
            in_specs.append(pl.BlockSpec(
                (1, S, LANES), lambda b, p, which=which: (b, 0, which * n_pairs + p)))
            args.append(arr)
    return pl.pallas_call(
        functools.partial(_attn_kernel, seq=S),
        grid=(B, n_pairs),
        in_specs=in_specs,
        out_specs=pl.BlockSpec((1, S, LANES), lambda b, p: (b, 0, p)),
        out_shape=jax.ShapeDtypeStruct((B, S, GROUP_WIDTH), BF16),
        scratch_shapes=[pltpu.VMEM((N_GROUPS, S, LANES), F32)] * 3
        + [pltpu.VMEM((3, WINDOW_STEPS, 2 * WINDOW_STEPS), F32)],
        compiler_params=_params(),
        name="dilated_attention",
    )(*args)


def _oproj_kernel(h_ref, o_ref, w_ref, gpost_ref, out_ref):
    m = jnp.dot(o_ref[0], w_ref[...], preferred_element_type=F32)
    out_ref[0] = h_ref[0] + _rms(m, gpost_ref[...])


def _oproj_layer(h, o, w, gpost, tm=512):
    B, S, D = h.shape
    return pl.pallas_call(
        _oproj_kernel,
        grid=(B, S // tm),
        in_specs=[_tile_spec(tm, D), _tile_spec(tm, GROUP_WIDTH), _const_spec(w.shape),
                  _const_spec((1, D))],
        out_specs=_tile_spec(tm, D),
        out_shape=jax.ShapeDtypeStruct(h.shape, F32),
        compiler_params=_params(),
        name="attn_out_proj",
    )(h, o, w, gpost)


def _rope_lane_constants():
    inv_freq = ROPE_THETA ** (-jnp.arange(0, ROT_DIM, 2, dtype=F32) / ROT_DIM)
    dim = jnp.arange(LANES) % HEAD_DIM
    half = ROT_DIM // 2
    invf = jnp.where(dim < ROT_DIM, inv_freq[dim % half], 0.0).astype(F32)
    sign = jnp.where(dim < half, -1.0, 1.0).astype(F32)
    return invf[None, :], sign[None, :]


def _prep_qkv_weight(w):
    q_cols = jnp.arange(w.shape[1]) < N_GROUPS * GROUP_WIDTH
    scale = jnp.where(q_cols, HEAD_DIM ** -0.5, 1.0).astype(F32)
    return (w * scale[None, :]).astype(BF16)


def kernel(x, positions, mix_norm_pre, mix_norm_post, ffn_norm_pre, ffn_norm_post,
           attn_w_qkv, attn_w_o, sc_w_in, sc_w_conv, sc_w_out,
           cc_w_pw1, cc_b_pw1, cc_w_dw, cc_b_dw, cc_ln_g, cc_ln_b, cc_w_pw2, cc_b_pw2,
           ffn_w_in, ffn_w_conv, ffn_w_out):
    B, S, D = x.shape
    row = lambda v: v.reshape(1, -1)

    ffn_win = ffn_w_in.astype(BF16)
    ffn_wconv = jnp.pad(ffn_w_conv, ((0, 0), (0, SUBLANES - 3), (0, 0)))
    ffn_wout = ffn_w_out.astype(BF16)

    invf_lane, sign_lane = _rope_lane_constants()
    pos_b = jnp.broadcast_to(positions.astype(F32)[:, :, None], (B, S, LANES))
    cos_t, sin_t = _rope_tables(pos_b, invf_lane, sign_lane)

    h = x
    for i in range(DEPTH):
        kind, j = i % N_MIXERS, i // N_MIXERS
        gpre, gpost = row(mix_norm_pre[i]), row(mix_norm_post[i])
        if kind == 0:
            qkvs = _qkv_layer(h, gpre, cos_t, sin_t, _prep_qkv_weight(attn_w_qkv[j]))
            h = _oproj_layer(h, _attn_layer(qkvs), attn_w_o[j].astype(BF16), gpost)
        elif kind == 1:
            h = _sc_layer(h, gpre, gpost, sc_w_in[j].astype(BF16),
                          jnp.pad(sc_w_conv[j], ((0, SUBLANES - 3), (0, 0))),
                          sc_w_out[j].astype(BF16))
        else:
            h = _cc_layer(h, gpre, gpost, cc_w_pw1[j].astype(BF16), row(cc_b_pw1[j]),
                          jnp.broadcast_to(cc_w_dw[j][:, None, :],
                                           (CONF_CONV_WIDTH, SUBLANES, D)),
                          row(cc_b_dw[j]), row(cc_ln_g[j]), row(cc_ln_b[j]),
                          cc_w_pw2[j].astype(BF16), row(cc_b_pw2[j]))
        h = _ffn_layer(h, row(ffn_norm_pre[i]), row(ffn_norm_post[i]),
                       ffn_win[i], ffn_wconv[i], ffn_wout[i])
    return h
```

```python
import functools

import jax
import jax.numpy as jnp
from jax import lax
from jax.experimental import pallas as pl
from jax.experimental.pallas import tpu as pltpu

F32 = jnp.float32
BF16 = jnp.bfloat16

D_MODEL = 1024
DEPTH = 4
N_MIXERS = 3
ATTN_GROUPS = ((128, 1), (512, 4), (2048, 16))
DILATIONS = tuple(d for _, d in ATTN_GROUPS)
N_GROUPS = len(ATTN_GROUPS)
HEADS_PER_GROUP = 16
HEAD_DIM = 64
ROT_DIM = HEAD_DIM // 4
ROPE_THETA = 500000.0
GROUP_WIDTH = HEADS_PER_GROUP * HEAD_DIM
WINDOW_STEPS = 128
CONF_CONV_WIDTH = 31
FFN_DIM = 2816
EPS_RMS = 1e-6
EPS_LN = 1e-5
NEG_INF = -1e30

LANES = 128
SUBLANES = 8
MXU_WIDTH = 256
FFN_CHUNK = MXU_WIDTH
N_FFN_CHUNKS = FFN_DIM // FFN_CHUNK
CONF_ROW_GROUPS = 4
CONF_COL_BLOCK = 512
VMEM_LIMIT = 60 * 1024 * 1024


def _rms(x, g):
    return x * lax.rsqrt(jnp.mean(x * x, axis=-1, keepdims=True) + EPS_RMS) * g


def _sigmoid(x):
    return 1.0 / (1.0 + jnp.exp(-x))


def _const_spec(shape):
    n = len(shape)
    return pl.BlockSpec(shape, lambda b, s: (0,) * n, pipeline_mode=pl.Buffered(1))


def _tile_spec(tm, width):
    return pl.BlockSpec((1, tm, width), lambda b, s: (b, s, 0))


def _params():
    return pltpu.CompilerParams(
        dimension_semantics=("arbitrary", "arbitrary"), vmem_limit_bytes=VMEM_LIMIT)


def _interleave(x, scratch_ref):
    p = x.shape[0] // SUBLANES
    n_lane_blocks = x.shape[1] // LANES
    for s in range(SUBLANES):
        for lb in range(n_lane_blocks):
            scratch_ref[lb, pl.ds(s, p, stride=SUBLANES), :] = (
                x[s * p:(s + 1) * p, lb * LANES:(lb + 1) * LANES])
    return jnp.concatenate([scratch_ref[lb] for lb in range(n_lane_blocks)], axis=1)


def _deinterleave(y, scratch_ref):
    p = y.shape[0] // SUBLANES
    n_lane_blocks = y.shape[1] // LANES
    for lb in range(n_lane_blocks):
        scratch_ref[lb] = y[:, lb * LANES:(lb + 1) * LANES]
    return jnp.concatenate(
        [jnp.concatenate([scratch_ref[lb, pl.ds(s, p, stride=SUBLANES), :]
                          for lb in range(n_lane_blocks)], axis=1)
         for s in range(SUBLANES)], axis=0)


def _history(prev_group, cur_group):
    first_sublane = lax.broadcasted_iota(jnp.int32, cur_group.shape, 0) == 0
    return jnp.where(first_sublane, pltpu.roll(prev_group, 1, 0), pltpu.roll(cur_group, 1, 0))


def _ffn_kernel(*refs, sub, n_sub, fuse_proj):
    if fuse_proj:
        h_ref, o_ref, wo_ref, gmix_ref = refs[:4]
        refs = refs[4:]
    else:
        h_ref = refs[0]
        refs = refs[1:]
    (gpre_ref, gpost_ref, win_ref, wconv_ref, wout_ref, out_ref,
     xn_ref, ybuf_ref, carry_ref, act_ref, perm_ref, unperm_ref) = refs

    @pl.when(pl.program_id(1) == 0)
    def _():
        carry_ref[...] = jnp.zeros_like(carry_ref)

    n_out_tiles = D_MODEL // MXU_WIDTH
    out_tile_after = [N_FFN_CHUNKS - 1 - 2 * j for j in range(n_out_tiles)][::-1]
    interleave = functools.partial(_interleave, scratch_ref=perm_ref)
    deinterleave = functools.partial(_deinterleave, scratch_ref=unperm_ref)
    history = _history

    def second_matmul_tile(j, tile):
        cols = slice(tile * MXU_WIDTH, (tile + 1) * MXU_WIDTH)
        f = None
        for k in range(N_FFN_CHUNKS):
            part = jnp.dot(act_ref[j % 2, k], wout_ref[k * FFN_CHUNK:(k + 1) * FFN_CHUNK, cols],
                           preferred_element_type=F32)
            f = part if f is None else f + part
        return f

    def finish(j, f_tiles):
        rows = slice(j * sub, (j + 1) * sub)
        f = jnp.concatenate(f_tiles, axis=1)
        base_ref = out_ref if fuse_proj else h_ref
        out_ref[0, rows, :] = base_ref[0, rows, :] + deinterleave(_rms(f, gpost_ref[...]))

    G = SUBLANES
    pending = None
    if fuse_proj:
        for j in range(n_sub):
            rows = slice(j * sub, (j + 1) * sub)
            m = jnp.dot(o_ref[0, rows, :], wo_ref[...], preferred_element_type=F32)
            out_ref[0, rows, :] = h_ref[0, rows, :] + _rms(m, gmix_ref[...])
    for j in range(n_sub):
        rows = slice(j * sub, (j + 1) * sub)
        x = out_ref[0, rows, :] if fuse_proj else h_ref[0, rows, :]
        xn_ref[j % 2] = interleave(_rms(x, gpre_ref[...])).astype(BF16)
        for c in range(N_FFN_CHUNKS):
            u = []
            for half in range(2):
                col = half * FFN_DIM + c * FFN_CHUNK
                ybuf = ybuf_ref.at[c % 2, half]
                taps = wconv_ref[:, col:col + FFN_CHUNK]
                y = jnp.dot(xn_ref[j % 2], win_ref[:, col:col + FFN_CHUNK],
                            preferred_element_type=F32)
                prev = carry_ref[c, half]
                last = y[sub - 2 * G:, :]
                ybuf[0:G, :] = history(prev[0:G], last[0:G])
                ybuf[G:2 * G, :] = history(prev[G:], last[G:])
                ybuf[2 * G:2 * G + sub, :] = y
                carry_ref[c, half] = last
                u.append(taps[0:1] * ybuf[0:sub, :] + taps[1:2] * ybuf[G:G + sub, :]
                         + taps[2:3] * y)
            act_ref[j % 2, c] = (u[0] * _sigmoid(u[0]) * u[1]).astype(BF16)
            if pending is not None and c in out_tile_after:
                pending[1].append(second_matmul_tile(pending[0], len(pending[1])))
        if pending is not None:
            finish(*pending)
        pending = (j, [])
    for tile in range(n_out_tiles):
        pending[1].append(second_matmul_tile(pending[0], tile))
    finish(*pending)


def _ffn_layer(h, gpre, gpost, win, wconv, wout, proj=None, tm=1024, sub=512):
    B, S, D = h.shape
    args, in_specs = [h], [_tile_spec(tm, D)]
    if proj is not None:
        o, wo, gmix = proj
        args += [o, wo, gmix]
        in_specs += [_tile_spec(tm, o.shape[-1]), _const_spec(wo.shape), _const_spec((1, D))]
    args += [gpre, gpost, win, wconv, wout]
    in_specs += [_const_spec((1, D)), _const_spec((1, D)), _const_spec(win.shape),
                 _const_spec(wconv.shape), _const_spec(wout.shape)]
    return pl.pallas_call(
        functools.partial(_ffn_kernel, sub=sub, n_sub=tm // sub, fuse_proj=proj is not None),
        grid=(B, S // tm),
        in_specs=in_specs,
        out_specs=_tile_spec(tm, D),
        out_shape=jax.ShapeDtypeStruct(h.shape, F32),
        scratch_shapes=[pltpu.VMEM((2, sub, D), BF16),
                        pltpu.VMEM((2, 2, sub + 2 * SUBLANES, FFN_CHUNK), F32),
                        pltpu.VMEM((N_FFN_CHUNKS, 2, 2 * SUBLANES, FFN_CHUNK), F32),
                        pltpu.VMEM((2, N_FFN_CHUNKS, sub, FFN_CHUNK), BF16),
                        pltpu.VMEM((D // LANES, sub, LANES), F32),
                        pltpu.VMEM((D // LANES, sub, LANES), F32)],
        compiler_params=_params(),
        name="conv_ffn",
    )(*args)


def _sc_kernel(h_ref, gpre_ref, gpost_ref, win_ref, wconv_ref, wout_ref, o_ref, ubuf_ref, *, tm):
    D = D_MODEL
    x = h_ref[0]
    xn = _rms(x, gpre_ref[...]).astype(BF16)

    @pl.when(pl.program_id(1) == 0)
    def _():
        ubuf_ref[0:SUBLANES, :] = jnp.zeros((SUBLANES, D), F32)

    p = jnp.dot(xn, win_ref[...], preferred_element_type=F32)
    u = p[:, D:2 * D] * p[:, 2 * D:]
    ubuf_ref[SUBLANES:SUBLANES + tm, :] = u
    taps = wconv_ref[...]
    conv = (taps[0:1] * ubuf_ref[SUBLANES - 2:SUBLANES - 2 + tm, :]
            + taps[1:2] * ubuf_ref[SUBLANES - 1:SUBLANES - 1 + tm, :]
            + taps[2:3] * u)
    ubuf_ref[0:SUBLANES, :] = ubuf_ref[tm:tm + SUBLANES, :]
    z = (p[:, :D] * conv).astype(BF16)
    m = jnp.dot(z, wout_ref[...], preferred_element_type=F32)
    o_ref[0] = x + _rms(m, gpost_ref[...])


def _sc_layer(h, gpre, gpost, win, wconv, wout, tm=512):
    B, S, D = h.shape
    return pl.pallas_call(
        functools.partial(_sc_kernel, tm=tm),
        grid=(B, S // tm),
        in_specs=[_tile_spec(tm, D), _const_spec((1, D)), _const_spec((1, D)),
                  _const_spec(win.shape), _const_spec(wconv.shape), _const_spec(wout.shape)],
        out_specs=_tile_spec(tm, D),
        out_shape=jax.ShapeDtypeStruct(h.shape, F32),
        scratch_shapes=[pltpu.VMEM((tm + SUBLANES, D), F32)],
        compiler_params=_params(),
        name="short_conv",
    )(h, gpre, gpost, win, wconv, wout)


def _cc_kernel(h_ref, gpre_ref, gpost_ref, w1_ref, b1_ref, wdw_ref, bdw_ref, lng_ref, lnb_ref,
               w2_ref, b2_ref, o_ref, e_ref, sw_ref, perm_ref, unperm_ref, *, sub, n_sub):
    D = D_MODEL
    K = CONF_CONV_WIDTH
    P = sub // SUBLANES
    G = CONF_ROW_GROUPS
    assert n_sub % 2 == 0

    @pl.when(pl.program_id(1) == 0)
    def _():
        e_ref[1, P:2 * P] = jnp.zeros((P, SUBLANES, D), F32)

    def glu_stage(j):
        rows = slice(j * sub, (j + 1) * sub)
        xn = _interleave(_rms(h_ref[0, rows, :], gpre_ref[...]), perm_ref.at[j % 2]).astype(BF16)
        p = jnp.dot(xn, w1_ref[...], preferred_element_type=F32) + b1_ref[...]
        glu = p[:, :D] * _sigmoid(p[:, D:])
        cur_e, prev_e = e_ref.at[j % 2], e_ref.at[(j - 1) % 2]
        for g in range(P):
            cur = glu[g * SUBLANES:(g + 1) * SUBLANES]
            cur_e[g] = _history(prev_e[P + g], cur)
            cur_e[P + g] = cur

    def conv_stage(j):
        cur_e = e_ref.at[j % 2]
        for g0 in range(0, P, G):
            blocks = []
            for c0 in range(0, D, CONF_COL_BLOCK):
                cols = slice(c0, c0 + CONF_COL_BLOCK)
                accs = [jnp.zeros((SUBLANES, CONF_COL_BLOCK), F32) + bdw_ref[:, cols]] * G
                for k in range(K):
                    tap = wdw_ref[k, :, cols]
                    first = P + g0 - (K - 1) + k
                    xs = cur_e[first:first + G, :, cols]
                    accs = [a + tap * xs[jj] for jj, a in enumerate(accs)]
                blocks.append(jnp.concatenate(accs, axis=0))
            acc = jnp.concatenate(blocks, axis=1)
            mu = jnp.mean(acc, axis=-1, keepdims=True)
            cen = acc - mu
            var = jnp.mean(cen * cen, axis=-1, keepdims=True)
            ln = cen * lax.rsqrt(var + EPS_LN) * lng_ref[...] + lnb_ref[...]
            sw_ref[j % 2, g0 * SUBLANES:(g0 + G) * SUBLANES, :] = (ln * _sigmoid(ln)).astype(BF16)

    def out_stage(j):
        rows = slice(j * sub, (j + 1) * sub)
        m = jnp.dot(sw_ref[j % 2], w2_ref[...], preferred_element_type=F32) + b2_ref[...]
        o_ref[0, rows, :] = h_ref[0, rows, :] + _deinterleave(
            _rms(m, gpost_ref[...]), unperm_ref.at[j % 2])

    glu_stage(0)
    for j in range(n_sub):
        if j + 1 < n_sub:
            glu_stage(j + 1)
        conv_stage(j)
        out_stage(j)


def _cc_layer(h, gpre, gpost, w1, b1, wdw, bdw, lng, lnb, w2, b2, tm=512, sub=256):
    B, S, D = h.shape
    vec = _const_spec((1, D))
    return pl.pallas_call(
        functools.partial(_cc_kernel, sub=sub, n_sub=tm // sub),
        grid=(B, S // tm),
        in_specs=[_tile_spec(tm, D), vec, vec, _const_spec(w1.shape), _const_spec(b1.shape),
                  _const_spec(wdw.shape), vec, vec, vec, _const_spec(w2.shape), vec],
        out_specs=_tile_spec(tm, D),
        out_shape=jax.ShapeDtypeStruct(h.shape, F32),
        scratch_shapes=[pltpu.VMEM((2, 2 * sub // SUBLANES, SUBLANES, D), F32),
                        pltpu.VMEM((2, sub, D), BF16),
                        pltpu.VMEM((2, D // LANES, sub, LANES), F32),
                        pltpu.VMEM((2, D // LANES, sub, LANES), F32)],
        compiler_params=_params(),
        name="conformer_conv",
    )(h, gpre, gpost, w1, b1, wdw, bdw, lng, lnb, w2, b2)


def _rope_table_kernel(pos_ref, invf_ref, sign_ref, cos_ref, sin_ref):
    ang = pos_ref[0] * invf_ref[...]
    cos_ref[0] = jnp.cos(ang)
    sin_ref[0] = jnp.sin(ang) * sign_ref[...]


def _rope_tables(pos_b, invf_lane, sign_lane, tm=512):
    B, S, _ = pos_b.shape
    out = jax.ShapeDtypeStruct(pos_b.shape, F32)
    return pl.pallas_call(
        _rope_table_kernel,
        grid=(B, S // tm),
        in_specs=[_tile_spec(tm, LANES), _const_spec((1, LANES)), _const_spec((1, LANES))],
        out_specs=[_tile_spec(tm, LANES), _tile_spec(tm, LANES)],
        out_shape=[out, out],
        compiler_params=_params(),
        name="rope_tables",
    )(pos_b, invf_lane, sign_lane)


def _qkv_kernel(h_ref, g_ref, cos_ref, sin_ref, w_ref, o0_ref, o1_ref, o2_ref,
                a_ref, ap_ref, cp_ref, sp_ref, *, tm):
    n_lane_blocks = D_MODEL // LANES
    a_nat = _rms(h_ref[0], g_ref[...])
    for j in range(n_lane_blocks):
        a_ref[j] = a_nat[:, j * LANES:(j + 1) * LANES]
    lane = lax.broadcasted_iota(jnp.int32, (1, LANES), 1)
    first_half = (lane % HEAD_DIM) < (ROT_DIM // 2)
    for g, (d, o_ref) in enumerate(zip(DILATIONS, (o0_ref, o1_ref, o2_ref))):
        rows = tm // d
        if d == 1:
            a = a_nat.astype(BF16)
            cos, sin = cos_ref[0], sin_ref[0]
        else:
            for r in range(d):
                dst = slice(r * rows, (r + 1) * rows)
                for j in range(n_lane_blocks):
                    ap_ref[dst, j * LANES:(j + 1) * LANES] = (
                        a_ref[j, pl.ds(r, rows, stride=d), :].astype(BF16))
                cp_ref[dst, :] = cos_ref[0, pl.ds(r, rows, stride=d), :]
                sp_ref[dst, :] = sin_ref[0, pl.ds(r, rows, stride=d), :]
            a = ap_ref[...]
            cos, sin = cp_ref[...], sp_ref[...]
        blocks_per_part = GROUP_WIDTH // LANES
        parts = []
        for which in range(3):
            c0 = (which * N_GROUPS + g) * GROUP_WIDTH
            parts.append(jnp.dot(a, w_ref[:, c0:c0 + GROUP_WIDTH], preferred_element_type=F32))
        blocks = []
        for j in range(3 * blocks_per_part):
            jj = j % blocks_per_part
            blk = parts[j // blocks_per_part][:, jj * LANES:(jj + 1) * LANES]
            if j < 2 * blocks_per_part:
                partner = jnp.where(first_half,
                                    pltpu.roll(blk, LANES - ROT_DIM // 2, 1),
                                    pltpu.roll(blk, ROT_DIM // 2, 1))
                blk = blk * cos + partner * sin
            blocks.append(blk.astype(BF16))
        out = jnp.concatenate(blocks, axis=1)
        for r in range(d):
            o_ref[0, r] = out[r * rows:(r + 1) * rows, :]


def _qkv_layer(h, g, cos_t, sin_t, w, tm=256):
    B, S, D = h.shape
    W3 = 3 * GROUP_WIDTH
    out_shapes = [jax.ShapeDtypeStruct((B, d, S // d, W3), BF16) for d in DILATIONS]
    out_specs = [pl.BlockSpec((1, d, tm // d, W3), lambda b, s: (b, 0, s, 0)) for d in DILATIONS]
    outs = pl.pallas_call(
        functools.partial(_qkv_kernel, tm=tm),
        grid=(B, S // tm),
        in_specs=[_tile_spec(tm, D), _const_spec((1, D)), _tile_spec(tm, LANES),
                  _tile_spec(tm, LANES), _const_spec(w.shape)],
        out_specs=out_specs,
        out_shape=out_shapes,
        scratch_shapes=[pltpu.VMEM((D // LANES, tm, LANES), F32), pltpu.VMEM((tm, D), BF16),
                        pltpu.VMEM((tm, LANES), F32), pltpu.VMEM((tm, LANES), F32)],
        compiler_params=_params(),
        name="qkv_rope",
    )(h, g, cos_t, sin_t, w)
    return [o.reshape(B, S, W3) for o in outs]


def _attn_kernel(*refs, seq):
    qkv_refs = refs[:3 * N_GROUPS]
    o_ref = refs[3 * N_GROUPS]
    acc_ref, m_ref, l_ref, bias_ref = refs[3 * N_GROUPS + 1:]
    W = WINDOW_STEPS
    lane = lax.broadcasted_iota(jnp.int32, (1, LANES), 1)
    head0 = lane < HEAD_DIM

    @pl.when((pl.program_id(0) == 0) & (pl.program_id(1) == 0))
    def _():
        qi = lax.broadcasted_iota(jnp.int32, (W, 2 * W), 0)
        kj = lax.broadcasted_iota(jnp.int32, (W, 2 * W), 1)
        dist = qi - kj + W
        in_window = (dist >= 0) & (dist <= W)
        for t, valid in enumerate((in_window, in_window & (kj >= W), kj <= qi)):
            bias_ref[t] = jnp.where(valid, 0.0, NEG_INF).astype(F32)

    for g, d in enumerate(DILATIONS):
        q_ref, k_ref, v_ref = qkv_refs[3 * g:3 * g + 3]
        blocks_per_sub = seq // d // W
        n_keys = W if blocks_per_sub == 1 else 2 * W

        def q_block(i, carry, q_ref=q_ref, k_ref=k_ref, v_ref=v_ref, d=d, g=g,
                    blocks_per_sub=blocks_per_sub, n_keys=n_keys):
            base = pl.multiple_of(i * W, W)
            r = i // blocks_per_sub
            n = i % blocks_per_sub
            if n_keys == W:
                kstart = base
                bias = bias_ref[2, :, 0:W]
            else:
                kstart = pl.multiple_of(jnp.maximum(base - W, 0), W)
                bias = bias_ref[jnp.where(i == 0, 2, jnp.where(n == 0, 1, 0))]
            q = q_ref[0, pl.ds(base, W), :]
            kk = k_ref[0, pl.ds(kstart, n_keys), :]
            vv = v_ref[0, pl.ds(kstart, n_keys), :]
            zero = jnp.zeros_like(q)
            q2 = jnp.concatenate([jnp.where(head0, q, zero), jnp.where(head0, zero, q)], axis=0)
            s2 = lax.dot_general(q2, kk, (((1,), (1,)), ((), ())), preferred_element_type=F32)
            ps, ms, ls = [], [], []
            for hd in range(2):
                s = s2[hd * W:(hd + 1) * W] + bias
                m = jnp.max(s, axis=1, keepdims=True)
                p = jnp.exp(s - m)
                ls.append(jnp.sum(p, axis=1, keepdims=True))
                ms.append(m)
                ps.append(p.astype(BF16))
            o2 = jnp.dot(jnp.concatenate(ps, axis=0), vv, preferred_element_type=F32)
            t0 = n * (W * d) + r
            dst = pl.ds(t0, W, stride=d) if d > 1 else pl.ds(t0, W)
            acc_ref[g, dst, :] = jnp.where(head0, o2[:W], o2[W:])
            m_ref[g, dst, :] = jnp.where(head0, ms[0], ms[1])
            l_ref[g, dst, :] = jnp.where(head0, ls[0], ls[1])
            return carry

        lax.fori_loop(0, seq // W, q_block, 0, unroll=16)

    def merge(i, carry):
        rows = pl.ds(pl.multiple_of(i * W, W), W)
        m_all = [m_ref[g, rows, :] for g in range(N_GROUPS)]
        m_max = functools.reduce(jnp.maximum, m_all)
        num = jnp.zeros((W, LANES), F32)
        den = jnp.zeros((W, LANES), F32)
        for g in range(N_GROUPS):
            wgt = jnp.exp(m_all[g] - m_max)
            num = num + wgt * acc_ref[g, rows, :]
            den = den + wgt * l_ref[g, rows, :]
        o_ref[0, rows, :] = (num / den).astype(BF16)
        return carry

    lax.fori_loop(0, seq // W, merge, 0, unroll=4)


def _attn_layer(qkvs):
    B, S, _ = qkvs[0].shape
    n_pairs = GROUP_WIDTH // LANES
    in_specs, args = [], []
    for arr in qkvs:
        for which in range(3):
            in_specs.append(pl.BlockSpec(
                (1, S, LANES), lambda b, p, which=which: (b, 0, which * n_pairs + p)))
            args.append(arr)
    return pl.pallas_call(
        functools.partial(_attn_kernel, seq=S),
        grid=(B, n_pairs),
        in_specs=in_specs,
        out_specs=pl.BlockSpec((1, S, LANES), lambda b, p: (b, 0, p)),
        out_shape=jax.ShapeDtypeStruct((B, S, GROUP_WIDTH), BF16),
        scratch_shapes=[pltpu.VMEM((N_GROUPS, S, LANES), F32)] * 3
        + [pltpu.VMEM((3, WINDOW_STEPS, 2 * WINDOW_STEPS), F32)],
        compiler_params=_params(),
        name="dilated_attention",
    )(*args)


def _rope_lane_constants():
    inv_freq = ROPE_THETA ** (-jnp.arange(0, ROT_DIM, 2, dtype=F32) / ROT_DIM)
    dim = jnp.arange(LANES) % HEAD_DIM
    half = ROT_DIM // 2
    invf = jnp.where(dim < ROT_DIM, inv_freq[dim % half], 0.0).astype(F32)
    sign = jnp.where(dim < half, -1.0, 1.0).astype(F32)
    return invf[None, :], sign[None, :]


def _prep_qkv_weight(w):
    q_cols = jnp.arange(w.shape[1]) < N_GROUPS * GROUP_WIDTH
    scale = jnp.where(q_cols, HEAD_DIM ** -0.5, 1.0).astype(F32)
    return (w * scale[None, :]).astype(BF16)


def kernel(x, positions, mix_norm_pre, mix_norm_post, ffn_norm_pre, ffn_norm_post,
           attn_w_qkv, attn_w_o, sc_w_in, sc_w_conv, sc_w_out,
           cc_w_pw1, cc_b_pw1, cc_w_dw, cc_b_dw, cc_ln_g, cc_ln_b, cc_w_pw2, cc_b_pw2,
           ffn_w_in, ffn_w_conv, ffn_w_out):
    B, S, D = x.shape
    row = lambda v: v.reshape(1, -1)

    ffn_win = ffn_w_in.astype(BF16)
    ffn_wconv = jnp.pad(ffn_w_conv, ((0, 0), (0, SUBLANES - 3), (0, 0)))
    ffn_wout = ffn_w_out.astype(BF16)

    invf_lane, sign_lane = _rope_lane_constants()
    pos_b = jnp.broadcast_to(positions.astype(F32)[:, :, None], (B, S, LANES))
    cos_t, sin_t = _rope_tables(pos_b, invf_lane, sign_lane)

    h = x
    for i in range(DEPTH):
        kind, j = i % N_MIXERS, i // N_MIXERS
        gpre, gpost = row(mix_norm_pre[i]), row(mix_norm_post[i])
        proj = None
        if kind == 0:
            qkvs = _qkv_layer(h, gpre, cos_t, sin_t, _prep_qkv_weight(attn_w_qkv[j]))
            proj = (_attn_layer(qkvs), attn_w_o[j].astype(BF16), gpost)
        elif kind == 1:
            h = _sc_layer(h, gpre, gpost, sc_w_in[j].astype(BF16),
                          jnp.pad(sc_w_conv[j], ((0, SUBLANES - 3), (0, 0))),
                          sc_w_out[j].astype(BF16))
        else:
            h = _cc_layer(h, gpre, gpost, cc_w_pw1[j].astype(BF16), row(cc_b_pw1[j]),
                          jnp.broadcast_to(cc_w_dw[j][:, None, :],
                                           (CONF_CONV_WIDTH, SUBLANES, D)),
                          row(cc_b_dw[j]), row(cc_ln_g[j]), row(cc_ln_b[j]),
                          cc_w_pw2[j].astype(BF16), row(cc_b_pw2[j]))
        h = _ffn_layer(h, row(ffn_norm_pre[i]), row(ffn_norm_post[i]),
                       ffn_win[i], ffn_wconv[i], ffn_wout[i], proj=proj)
    return h
```
